```python
import math
import jax, jax.numpy as jnp
from jax import lax
import numpy as np

D_MODEL = 1024
BATCH = 8
SEQ = 4096
DEPTH = 2

N_MEM = 256
N_MIXERS = 2
D_MIX = D_MODEL
XA_HEADS = 4
XA_HEAD_DIM = D_MODEL // 16
D_XA = XA_HEADS * XA_HEAD_DIM
D_TOK = D_MIX - D_XA
CONV_WIDTH = 3
ML_HEADS = 4
ML_HEAD_DIM = D_TOK // ML_HEADS
ML_CHUNK = 64
QK_CONV_WIDTH = 4
D_FF = 256 * int(math.ceil(8 * D_MODEL / 3 / 256))
LN_EPS = 1e-5
DEEPNORM_ALPHA = (2.0 * DEPTH) ** 0.25
DEEPNORM_BETA = (8.0 * DEPTH) ** -0.25
N_CONV_LAYERS = (DEPTH + 1) // 2
N_MLSTM_LAYERS = DEPTH // 2
D_IN_CONV = 3 * D_TOK + D_XA
D_IN_MLSTM = 4 * D_TOK + 2 * ML_HEADS + D_XA

kernel_name = "hybrid_shortconv_mlstm_memxattn_macaron_deepnorm"


def layer_norm(x, g, b):
    xf = x.astype(jnp.float32)
    mu = jnp.mean(xf, -1, keepdims=True)
    var = jnp.mean(jnp.square(xf - mu), -1, keepdims=True)
    y = (xf - mu) * lax.rsqrt(var + LN_EPS)
    return (y * g.astype(jnp.float32) + b.astype(jnp.float32)).astype(x.dtype)


def swiglu(x, w_gate, w_up, w_down):
    return (jax.nn.silu(x @ w_gate) * (x @ w_up)) @ w_down


def causal_dwconv(u, w):
    width = w.shape[0]
    s = u.shape[1]
    up = jnp.pad(u, ((0, 0), (width - 1, 0), (0, 0)))
    out = up[:, width - 1:width - 1 + s] * w[width - 1]
    for j in range(width - 1):
        out = out + up[:, j:j + s] * w[j]
    return out


def memory_cross_attention(q, mem_kv):
    b, s, _ = q.shape
    q = q.reshape(b, s, XA_HEADS, XA_HEAD_DIM)
    k, v = jnp.split(mem_kv, 2, -1)
    k = k.reshape(b, N_MEM, XA_HEADS, XA_HEAD_DIM)
    v = v.reshape(b, N_MEM, XA_HEADS, XA_HEAD_DIM)
    scores = jnp.einsum('bshd,bmhd->bhsm', q, k).astype(jnp.float32) * (XA_HEAD_DIM ** -0.5)
    p = jax.nn.softmax(scores, -1).astype(v.dtype)
    o = jnp.einsum('bhsm,bmhd->bshd', p, v)
    return o.reshape(b, s, D_XA)


def short_conv_mixer(u, conv_w):
    b_gate, c_gate, x_in = jnp.split(u, 3, -1)
    return b_gate * causal_dwconv(c_gate * x_in, conv_w)


def mlstm_chunkwise(q, k, v, log_i, log_f):
    b, s, _ = q.shape
    h_, dh, L = ML_HEADS, ML_HEAD_DIM, ML_CHUNK
    nc = s // L

    def to_chunks(t):
        return t.astype(jnp.float32).reshape(b, nc, L, h_, dh).transpose(1, 0, 3, 2, 4)

    def gate_chunks(g):
        return g.astype(jnp.float32).reshape(b, nc, L, h_).transpose(1, 0, 3, 2)

    qc, kc, vc = to_chunks(q), to_chunks(k) * (dh ** -0.5), to_chunks(v)
    ic, fc = gate_chunks(log_i), gate_chunks(log_f)
    causal = jnp.tril(jnp.ones((L, L), dtype=bool))

    def step(carry, xs):
        c_st, n_st, m_st = carry
        q_, k_, v_, li, lf = xs
        bcum = jnp.cumsum(lf, -1)
        log_d = bcum[..., :, None] - bcum[..., None, :] + li[..., None, :]
        log_d = jnp.where(causal, log_d, -jnp.inf)
        log_inter = bcum + m_st[..., None]
        m_t = jnp.maximum(log_inter, jnp.max(log_d, -1))
        w_intra = jnp.exp(log_d - m_t[..., None])
        w_inter = jnp.exp(log_inter - m_t)
        sc = jnp.einsum('bhtd,bhsd->bhts', q_, k_) * w_intra
        num = (jnp.einsum('bhts,bhse->bhte', sc, v_)
               + w_inter[..., None] * jnp.einsum('bhtd,bhde->bhte', q_, c_st))
        den = jnp.sum(sc, -1) + w_inter * jnp.einsum('bhtd,bhd->bht', q_, n_st)
        h = num / jnp.maximum(jnp.abs(den), jnp.exp(-m_t))[..., None]
        b_last = bcum[..., -1]
        log_w = b_last[..., None] - bcum + li
        m_new = jnp.maximum(b_last + m_st, jnp.max(log_w, -1))
        w_k = jnp.exp(log_w - m_new[..., None])
        decay = jnp.exp(b_last + m_st - m_new)
        c_new = decay[..., None, None] * c_st + jnp.einsum('bhs,bhsd,bhse->bhde', w_k, k_, v_)
        n_new = decay[..., None] * n_st + jnp.einsum('bhs,bhsd->bhd', w_k, k_)
        return (c_new, n_new, m_new), h

    init = (jnp.zeros((b, h_, dh, dh), jnp.float32),
            jnp.zeros((b, h_, dh), jnp.float32),
            jnp.zeros((b, h_), jnp.float32))
    _, hs = lax.scan(step, init, (qc, kc, vc, ic, fc))
    return hs.transpose(1, 0, 3, 2, 4).reshape(b, s, h_, dh)


def mlstm_mixer(u, b_gates, qk_conv_w, head_norm_g):
    b, s, _ = u.shape
    qk, v, o_pre, gates = jnp.split(u, [2 * D_TOK, 3 * D_TOK, 4 * D_TOK], -1)
    qk = jax.nn.silu(causal_dwconv(qk, qk_conv_w))
    q, k = jnp.split(qk, 2, -1)
    gates = gates.astype(jnp.float32) + b_gates.astype(jnp.float32)
    log_i = gates[..., :ML_HEADS]
    log_f = jax.nn.log_sigmoid(gates[..., ML_HEADS:])
    h = mlstm_chunkwise(q, k, v, log_i, log_f)
    mu = jnp.mean(h, -1, keepdims=True)
    var = jnp.mean(jnp.square(h - mu), -1, keepdims=True)
    h = (h - mu) * lax.rsqrt(var + LN_EPS) * head_norm_g.astype(jnp.float32)
    return jax.nn.sigmoid(o_pre) * h.reshape(b, s, D_TOK).astype(u.dtype)


def setup_inputs(seed: int = 0) -> dict:
    key = jax.random.key(seed)
    ks = jax.random.split(key, 20)
    nrm = jax.random.normal
    f32 = jnp.float32
    x = nrm(ks[0], (BATCH, SEQ, D_MODEL), f32)
    mem = nrm(ks[1], (BATCH, N_MEM, D_MODEL), f32)
    ln_g = 1.0 + 0.02 * nrm(ks[2], (DEPTH, 3, D_MODEL), f32)
    ln_b = 0.02 * nrm(ks[3], (DEPTH, 3, D_MODEL), f32)
    ffn_w_gate = nrm(ks[4], (DEPTH, 2, D_MODEL, D_FF), f32) * D_MODEL ** -0.5
    ffn_w_up = nrm(ks[5], (DEPTH, 2, D_MODEL, D_FF), f32) * D_MODEL ** -0.5
    ffn_w_down = nrm(ks[6], (DEPTH, 2, D_FF, D_MODEL), f32) * (D_FF ** -0.5 * DEEPNORM_BETA)
    w_kv_mem = nrm(ks[7], (DEPTH, D_MODEL, 2 * D_XA), f32) * D_MODEL ** -0.5
    w_out = nrm(ks[8], (DEPTH, D_MIX, D_MODEL), f32) * (D_MIX ** -0.5 * DEEPNORM_BETA)
    w_in_conv = nrm(ks[9], (N_CONV_LAYERS, D_MODEL, D_IN_CONV), f32) * D_MODEL ** -0.5
    conv_w = nrm(ks[10], (N_CONV_LAYERS, CONV_WIDTH, D_TOK), f32) * CONV_WIDTH ** -0.5
    w_in_mlstm = nrm(ks[11], (N_MLSTM_LAYERS, D_MODEL, D_IN_MLSTM), f32) * D_MODEL ** -0.5
    w_in_mlstm = w_in_mlstm.at[:, :, 4 * D_TOK:4 * D_TOK + 2 * ML_HEADS].multiply(0.1)
    b_i = 0.1 * nrm(ks[12], (N_MLSTM_LAYERS, ML_HEADS), f32)
    b_f = jnp.linspace(3.0, 6.0, ML_HEADS, dtype=f32) + 0.1 * nrm(ks[13], (N_MLSTM_LAYERS, ML_HEADS), f32)
    b_gates = jnp.concatenate([b_i, b_f], -1)
    qk_conv_w = nrm(ks[14], (N_MLSTM_LAYERS, QK_CONV_WIDTH, 2 * D_TOK), f32) * QK_CONV_WIDTH ** -0.5
    head_norm_g = 1.0 + 0.02 * nrm(ks[15], (N_MLSTM_LAYERS, ML_HEADS, ML_HEAD_DIM), f32)
    return {"x": x, "mem": mem, "ln_g": ln_g, "ln_b": ln_b,
            "ffn_w_gate": ffn_w_gate, "ffn_w_up": ffn_w_up, "ffn_w_down": ffn_w_down,
            "w_kv_mem": w_kv_mem, "w_out": w_out,
            "w_in_conv": w_in_conv, "conv_w": conv_w,
            "w_in_mlstm": w_in_mlstm, "b_gates": b_gates,
            "qk_conv_w": qk_conv_w, "head_norm_g": head_norm_g}


def reference(x, mem, ln_g, ln_b, ffn_w_gate, ffn_w_up, ffn_w_down, w_kv_mem, w_out,
              w_in_conv, conv_w, w_in_mlstm, b_gates, qk_conv_w, head_norm_g):
    alpha = DEEPNORM_ALPHA
    for l in range(DEPTH):
        x = layer_norm(alpha * x + 0.5 * swiglu(x, ffn_w_gate[l, 0], ffn_w_up[l, 0], ffn_w_down[l, 0]),
                       ln_g[l, 0], ln_b[l, 0])
        mem_kv = mem @ w_kv_mem[l]
        j = l // N_MIXERS
        if l % N_MIXERS == 0:
            u = x @ w_in_conv[j]
            tok = short_conv_mixer(u[..., :3 * D_TOK], conv_w[j])
        else:
            u = x @ w_in_mlstm[j]
            tok = mlstm_mixer(u[..., :4 * D_TOK + 2 * ML_HEADS], b_gates[j], qk_conv_w[j], head_norm_g[j])
        xa = memory_cross_attention(u[..., -D_XA:], mem_kv)
        mix = jnp.concatenate([tok, xa], -1) @ w_out[l]
        x = layer_norm(alpha * x + mix, ln_g[l, 1], ln_b[l, 1])
        x = layer_norm(alpha * x + 0.5 * swiglu(x, ffn_w_gate[l, 1], ffn_w_up[l, 1], ffn_w_down[l, 1]),
                       ln_g[l, 2], ln_b[l, 2])
    return x
```

```python
import functools

import jax
import jax.numpy as jnp
from jax import lax
from jax.experimental import pallas as pl
from jax.experimental.pallas import tpu as pltpu

F32 = jnp.float32
BF16 = jnp.bfloat16

XA_HEADS = 4
CONV_WIDTH = 3
QK_CONV_WIDTH = 4
ML_HEADS = 4
LN_EPS = 1e-5

LANES = 128
SUBLANES = 8
MXU_DIM = 256
VMEM_LIMIT_BYTES = 56 * 1024 * 1024

TM_FFN = 512
TM_MIX = 512
ML_CHUNK = MXU_DIM
GATE_PAD = LANES
NEG_BIG = -1e30


def _layer_norm(r, g, b):
    mu = jnp.mean(r, axis=-1, keepdims=True)
    d = r - mu
    var = jnp.mean(d * d, axis=-1, keepdims=True)
    return d * lax.rsqrt(var + LN_EPS) * g + b


def _sigmoid(x):
    return 1.0 / (1.0 + jnp.exp(-x))


def _dot(a, b):
    return jnp.dot(a, b, preferred_element_type=F32)


def _const_spec(shape):
    nd = len(shape)
    return pl.BlockSpec(shape, lambda *_: (0,) * nd)


def _ffn_ln_kernel(alpha, x_ref, wg_ref, wu_ref, wd_ref, g_ref, b_ref, o_ref):
    x = x_ref[...]
    xb = x.astype(BF16)
    gate = _dot(xb, wg_ref[...])
    up = _dot(xb, wu_ref[...])
    h = (gate * _sigmoid(gate) * up).astype(BF16)
    y = _dot(h, wd_ref[...])
    o_ref[...] = _layer_norm(alpha * x + 0.5 * y, g_ref[...], b_ref[...])


def _ffn_ln(x2, wg, wu, wd, g, b, alpha):
    n, d = x2.shape
    d_ff = wg.shape[1]
    tm = min(TM_FFN, n)
    return pl.pallas_call(
        functools.partial(_ffn_ln_kernel, alpha),
        grid=(n // tm,),
        in_specs=[
            pl.BlockSpec((tm, d), lambda i: (i, 0)),
            _const_spec((d, d_ff)),
            _const_spec((d, d_ff)),
            _const_spec((d_ff, d)),
            _const_spec((1, d)),
            _const_spec((1, d)),
        ],
        out_specs=pl.BlockSpec((tm, d), lambda i: (i, 0)),
        out_shape=jax.ShapeDtypeStruct((n, d), F32),
        compiler_params=pltpu.CompilerParams(
            dimension_semantics=("arbitrary",), vmem_limit_bytes=VMEM_LIMIT_BYTES),
    )(x2, wg, wu, wd, g, b)


def _mem_kv_kernel(scale, mem_ref, wkt_ref, wv_ref, kbd_ref, vbd_ref):
    memb = mem_ref[0].astype(BF16)
    n_mem = memb.shape[0]
    d_xa = wv_ref.shape[-1]
    hd = d_xa // XA_HEADS
    kt = lax.dot_general(wkt_ref[0], memb, (((1,), (1,)), ((), ())),
                         preferred_element_type=F32) * scale
    v = _dot(memb, wv_ref[0])
    kt4 = jnp.concatenate([kt] * XA_HEADS, axis=1)
    r = lax.broadcasted_iota(jnp.int32, kt4.shape, 0) // hd
    c = lax.broadcasted_iota(jnp.int32, kt4.shape, 1) // n_mem
    kbd_ref[0, 0] = jnp.where(r == c, kt4, 0.0).astype(BF16)
    v4 = jnp.concatenate([v] * XA_HEADS, axis=0)
    r = lax.broadcasted_iota(jnp.int32, v4.shape, 0) // n_mem
    c = lax.broadcasted_iota(jnp.int32, v4.shape, 1) // hd
    vbd_ref[0, 0] = jnp.where(r == c, v4, 0.0).astype(BF16)


def _mem_kv(mem, wkt, wv):
    b, n_mem, d = mem.shape
    depth, d_xa, _ = wkt.shape
    hd = d_xa // XA_HEADS
    return pl.pallas_call(
        functools.partial(_mem_kv_kernel, hd ** -0.5),
        grid=(depth, b),
        in_specs=[
            pl.BlockSpec((1, n_mem, d), lambda l, i: (i, 0, 0)),
            pl.BlockSpec((1, d_xa, d), lambda l, i: (l, 0, 0)),
            pl.BlockSpec((1, d, d_xa), lambda l, i: (l, 0, 0)),
        ],
        out_specs=[
            pl.BlockSpec((1, 1, d_xa, XA_HEADS * n_mem), lambda l, i: (l, i, 0, 0)),
            pl.BlockSpec((1, 1, XA_HEADS * n_mem, d_xa), lambda l, i: (l, i, 0, 0)),
        ],
        out_shape=[
            jax.ShapeDtypeStruct((depth, b, d_xa, XA_HEADS * n_mem), BF16),
            jax.ShapeDtypeStruct((depth, b, XA_HEADS * n_mem, d_xa), BF16),
        ],
        compiler_params=pltpu.CompilerParams(
            dimension_semantics=("arbitrary", "arbitrary"), vmem_limit_bytes=VMEM_LIMIT_BYTES),
    )(mem, wkt, wv)


def _cross_attention(q, kbd, vbd):
    tm, d_xa = q.shape
    n_mem = kbd.shape[1] // XA_HEADS
    hd = d_xa // XA_HEADS
    s = _dot(q.astype(BF16), kbd)
    ps, ls = [], []
    for h in range(XA_HEADS):
        sh = s[:, h * n_mem:(h + 1) * n_mem]
        p = jnp.exp(sh - jnp.max(sh, axis=-1, keepdims=True))
        ps.append(p.astype(BF16))
        ls.append(jnp.sum(p, axis=-1, keepdims=True))
    o = _dot(jnp.concatenate(ps, axis=1), vbd)
    lane_head = lax.broadcasted_iota(jnp.int32, (tm, d_xa), 1) // hd
    l = jnp.broadcast_to(ls[XA_HEADS - 1], (tm, d_xa))
    for h in range(XA_HEADS - 2, -1, -1):
        l = jnp.where(lane_head == h, ls[h], l)
    return o / l


def _causal_taps(buf_ref, cur, w_ref, tm):
    width = w_ref.shape[0]
    buf_ref[pl.ds(SUBLANES, tm), :] = cur
    out = cur * w_ref[width - 1:width, :]
    for j in range(width - 1):
        off = SUBLANES - (width - 1) + j
        out = out + buf_ref[pl.ds(off, tm), :] * w_ref[j:j + 1, :]
    buf_ref[pl.ds(0, SUBLANES), :] = buf_ref[pl.ds(tm, SUBLANES), :]
    return out


def _mix_conv_kernel(alpha, x_ref, win_ref, convw_ref, kbd_ref, vbd_ref, wout_ref, g_ref, b_ref,
                     o_ref, zbuf_ref):
    tm = x_ref.shape[0]
    d_tok = convw_ref.shape[1]

    @pl.when(pl.program_id(1) == 0)
    def _():
        zbuf_ref[pl.ds(0, SUBLANES), :] = jnp.zeros((SUBLANES, d_tok), F32)

    x = x_ref[...]
    u = _dot(x.astype(BF16), win_ref[...])
    b_gate = u[:, :d_tok]
    z = u[:, d_tok:2 * d_tok] * u[:, 2 * d_tok:3 * d_tok]
    tok = b_gate * _causal_taps(zbuf_ref, z, convw_ref, tm)
    xa = _cross_attention(u[:, 3 * d_tok:], kbd_ref[0, 0], vbd_ref[0, 0])
    cat = jnp.concatenate([tok.astype(BF16), xa.astype(BF16)], axis=1)
    mix = _dot(cat, wout_ref[...])
    o_ref[...] = _layer_norm(alpha * x + mix, g_ref[...], b_ref[...])


def _mix_conv(x2, batch, win, convw, kbd, vbd, layer, wout, g, b, alpha):
    n, d = x2.shape
    seq = n // batch
    tm = min(TM_MIX, seq)
    nt = seq // tm
    d_tok = convw.shape[1]
    kv_shape = kbd.shape
    return pl.pallas_call(
        functools.partial(_mix_conv_kernel, alpha),
        grid=(batch, nt),
        in_specs=[
            pl.BlockSpec((tm, d), lambda i, t: (i * nt + t, 0)),
            _const_spec(win.shape),
            _const_spec(convw.shape),
            pl.BlockSpec((1, 1) + kv_shape[2:], lambda i, t: (layer, i, 0, 0)),
            pl.BlockSpec((1, 1) + vbd.shape[2:], lambda i, t: (layer, i, 0, 0)),
            _const_spec(wout.shape),
            _const_spec((1, d)),
            _const_spec((1, d)),
        ],
        out_specs=pl.BlockSpec((tm, d), lambda i, t: (i * nt + t, 0)),
        out_shape=jax.ShapeDtypeStruct((n, d), F32),
        scratch_shapes=[pltpu.VMEM((tm + SUBLANES, d_tok), F32)],
        compiler_params=pltpu.CompilerParams(
            dimension_semantics=("arbitrary", "arbitrary"), vmem_limit_bytes=VMEM_LIMIT_BYTES),
    )(x2, win, convw, kbd, vbd, wout, g, b)


def _head_window(h, dh):
    pair_start = (h // 2) * 2 * dh
    if h % 2 == 0:
        return pair_start, 0, dh
    return pair_start + 2 * dh - MXU_DIM, MXU_DIM - dh, 0


def _mlstm_chunk_head(qw, kw, vaug, icol, bcol, irow, brow, b_last, tri, c_ref, m_ref, h, spare_lane):
    length, width = vaug.shape
    m = m_ref[h, 0:1, 0:1]
    log_d = jnp.where(tri, bcol - brow + irow, NEG_BIG)
    log_inter = bcol + m
    m_t = jnp.maximum(log_inter, jnp.max(log_d, axis=-1, keepdims=True))
    w_intra = jnp.exp(log_d - m_t)
    w_inter = jnp.exp(log_inter - m_t)
    s = lax.dot_general(qw, kw, (((1,), (1,)), ((), ())), preferred_element_type=F32)
    sc = (s * w_intra).astype(BF16)
    c_st = c_ref[h]
    num = _dot(sc, vaug.astype(BF16)) + w_inter * _dot(qw, c_st.astype(BF16))
    lane = lax.broadcasted_iota(jnp.int32, (length, width), 1)
    den = jnp.sum(jnp.where(lane == spare_lane, num, 0.0), axis=-1, keepdims=True)
    h_out = num / jnp.maximum(jnp.abs(den), jnp.exp(-m_t))
    log_w_row = b_last - brow + irow
    m_new = jnp.maximum(b_last + m, jnp.max(log_w_row, axis=-1, keepdims=True))
    w_k = jnp.exp(b_last - bcol + icol - m_new)
    decay = jnp.exp(b_last + m - m_new)
    upd = lax.dot_general(kw, (w_k * vaug).astype(BF16), (((0,), (0,)), ((), ())),
                          preferred_element_type=F32)
    c_ref[h] = decay * c_st + upd
    m_ref[h] = jnp.broadcast_to(m_new, m_ref.shape[1:])
    return h_out


def _mix_mlstm_kernel(alpha, x_ref, win_ref, convw_ref, bg_ref, hng_ref, kbd_ref, vbd_ref, wout_ref,
                      g_ref, b_ref, o_ref, cbuf_ref, c_ref, m_ref):
    tm = x_ref.shape[0]
    d_qk = convw_ref.shape[1]
    d_tok = d_qk // 2
    dh = d_tok // ML_HEADS
    length = ML_CHUNK
    width = MXU_DIM

    @pl.when(pl.program_id(1) == 0)
    def _():
        cbuf_ref[pl.ds(0, SUBLANES), :] = jnp.zeros((SUBLANES, d_qk), F32)
        c_ref[...] = jnp.zeros(c_ref.shape, F32)
        m_ref[...] = jnp.zeros(m_ref.shape, F32)

    x = x_ref[...]
    u = _dot(x.astype(BF16), win_ref[...])
    qk = _causal_taps(cbuf_ref, u[:, :d_qk], convw_ref, tm)
    qk = qk * _sigmoid(qk)
    q = qk[:, :d_tok].astype(BF16)
    k = qk[:, d_tok:] * (dh ** -0.5)
    v = u[:, d_qk:d_qk + d_tok]
    o_pre = u[:, d_qk + d_tok:d_qk + 2 * d_tok]
    gates = u[:, d_qk + 2 * d_tok:d_qk + 2 * d_tok + GATE_PAD] + bg_ref[...]
    q_mem = u[:, d_qk + 2 * d_tok + GATE_PAD:]

    glane = lax.broadcasted_iota(jnp.int32, gates.shape, 1)
    log_sig = jnp.minimum(gates, 0.0) - jnp.log(1.0 + jnp.exp(-jnp.abs(gates)))
    gl = jnp.where(glane < ML_HEADS, gates, log_sig)

    row = lax.broadcasted_iota(jnp.int32, (length, length), 0)
    col = lax.broadcasted_iota(jnp.int32, (length, length), 1)
    tri = row >= col
    tri_f = tri.astype(F32)
    wlane = lax.broadcasted_iota(jnp.int32, (length, width), 1)

    hn_rows = []
    for c in range(tm // length):
        rows = slice(c * length, (c + 1) * length)
        gc = gl[rows]
        cum = jnp.dot(tri_f, gc, preferred_element_type=F32,
                      precision=lax.Precision.HIGHEST)
        gct = gc.T
        cumt = cum.T
        wins = []
        for h in range(ML_HEADS):
            start, first, spare = _head_window(h, dh)
            valid = (wlane >= first) & (wlane < first + dh)
            qw = q[rows, start:start + width]
            kw = jnp.where(valid, k[rows, start:start + width], 0.0).astype(BF16)
            vaug = jnp.where(wlane == spare, 1.0, v[rows, start:start + width])
            icol = gc[:, h:h + 1]
            bcol = cum[:, ML_HEADS + h:ML_HEADS + h + 1]
            irow = gct[h:h + 1, :]
            brow = cumt[ML_HEADS + h:ML_HEADS + h + 1, :]
            b_last = cum[length - 1:length, ML_HEADS + h:ML_HEADS + h + 1]
            hw = _mlstm_chunk_head(qw, kw, vaug, icol, bcol, irow, brow, b_last, tri,
                                   c_ref, m_ref, h, spare)
            mu = jnp.sum(jnp.where(valid, hw, 0.0), axis=-1, keepdims=True) / dh
            d = jnp.where(valid, hw - mu, 0.0)
            var = jnp.sum(d * d, axis=-1, keepdims=True) / dh
            wins.append(d * lax.rsqrt(var + LN_EPS))
        pieces = []
        for p in range(ML_HEADS // 2):
            even, odd = wins[2 * p], wins[2 * p + 1]
            pieces += [even[:, :LANES], even[:, LANES:] + odd[:, :LANES], odd[:, LANES:]]
        hn_rows.append(jnp.concatenate(pieces, axis=1))
    hn = hn_rows[0] if len(hn_rows) == 1 else jnp.concatenate(hn_rows, axis=0)

    tok = _sigmoid(o_pre) * (hn * hng_ref[...])
    xa = _cross_attention(q_mem, kbd_ref[0, 0], vbd_ref[0, 0])
    cat = jnp.concatenate([tok.astype(BF16), xa.astype(BF16)], axis=1)
    mix = _dot(cat, wout_ref[...])
    o_ref[...] = _layer_norm(alpha * x + mix, g_ref[...], b_ref[...])


def _mix_mlstm(x2, batch, win, convw, bg, hng, kbd, vbd, layer, wout, g, b, alpha):
    n, d = x2.shape
    seq = n // batch
    tm = min(TM_MIX, seq)
    nt = seq // tm
    d_qk = convw.shape[1]
    return pl.pallas_call(
        functools.partial(_mix_mlstm_kernel, alpha),
        grid=(batch, nt),
        in_specs=[
            pl.BlockSpec((tm, d), lambda i, t: (i * nt + t, 0)),
            _const_spec(win.shape),
            _const_spec(convw.shape),
            _const_spec(bg.shape),
            _const_spec(hng.shape),
            pl.BlockSpec((1, 1) + kbd.shape[2:], lambda i, t: (layer, i, 0, 0)),
            pl.BlockSpec((1, 1) + vbd.shape[2:], lambda i, t: (layer, i, 0, 0)),
            _const_spec(wout.shape),
            _const_spec((1, d)),
            _const_spec((1, d)),
        ],
        out_specs=pl.BlockSpec((tm, d), lambda i, t: (i * nt + t, 0)),
        out_shape=jax.ShapeDtypeStruct((n, d), F32),
        scratch_shapes=[
            pltpu.VMEM((tm + SUBLANES, d_qk), F32),
            pltpu.VMEM((ML_HEADS, MXU_DIM, MXU_DIM), F32),
            pltpu.VMEM((ML_HEADS, SUBLANES, LANES), F32),
        ],
        compiler_params=pltpu.CompilerParams(
            dimension_semantics=("arbitrary", "arbitrary"), vmem_limit_bytes=VMEM_LIMIT_BYTES),
    )(x2, win, convw, bg, hng, kbd, vbd, wout, g, b)


def kernel(x, mem, ln_g, ln_b, ffn_w_gate, ffn_w_up, ffn_w_down, w_kv_mem, w_out, w_in_conv, conv_w,
           w_in_mlstm, b_gates, qk_conv_w, head_norm_g):
    batch, seq, d = x.shape
    depth = ffn_w_gate.shape[0]
    d_xa = w_kv_mem.shape[-1] // 2
    d_tok = conv_w.shape[-1]
    n_mixers = 2
    alpha = (2.0 * depth) ** 0.25
    assert seq % ML_CHUNK == 0 and d_tok % ML_HEADS == 0
    assert 2 * (d_tok // ML_HEADS) % LANES == 0 and d_tok // ML_HEADS <= MXU_DIM

    wg = ffn_w_gate.astype(BF16)
    wu = ffn_w_up.astype(BF16)
    wd = ffn_w_down.astype(BF16)
    wo = w_out.astype(BF16)
    wkt = jnp.swapaxes(w_kv_mem[:, :, :d_xa], 1, 2).astype(BF16)
    wv = w_kv_mem[:, :, d_xa:].astype(BF16)
    kbd, vbd = _mem_kv(mem, wkt, wv)

    x2 = x.reshape(batch * seq, d)
    for l in range(depth):
        g = ln_g[l][:, None, :]
        b = ln_b[l][:, None, :]
        x2 = _ffn_ln(x2, wg[l, 0], wu[l, 0], wd[l, 0], g[0], b[0], alpha)
        j = l // n_mixers
        if l % n_mixers == 0:
            x2 = _mix_conv(x2, batch, w_in_conv[j].astype(BF16), conv_w[j], kbd, vbd, l, wo[l],
                           g[1], b[1], alpha)
        else:
            w = w_in_mlstm[j]
            n_gate = 2 * ML_HEADS
            gate_w = jnp.pad(w[:, 4 * d_tok:4 * d_tok + n_gate], ((0, 0), (0, GATE_PAD - n_gate)))
            win = jnp.concatenate([w[:, :4 * d_tok], gate_w, w[:, 4 * d_tok + n_gate:]],
                                  axis=1).astype(BF16)
            bg = jnp.pad(b_gates[j], (0, GATE_PAD - n_gate))[None, :]
            hng = head_norm_g[j].reshape(1, d_tok)
            x2 = _mix_mlstm(x2, batch, win, qk_conv_w[j], bg, hng, kbd, vbd, l, wo[l],
                            g[1], b[1], alpha)
        x2 = _ffn_ln(x2, wg[l, 1], wu[l, 1], wd[l, 1], g[2], b[2], alpha)
    return x2.reshape(batch, seq, d)
```

```python
import functools

import jax
import jax.numpy as jnp
from jax import lax
from jax.experimental import pallas as pl
from jax.experimental.pallas import tpu as pltpu

F32 = jnp.float32
BF16 = jnp.bfloat16

XA_HEADS = 4
CONV_WIDTH = 3
QK_CONV_WIDTH = 4
ML_HEADS = 4
LN_EPS = 1e-5

LANES = 128
SUBLANES = 8
MXU_DIM = 256
VMEM_LIMIT_BYTES = 56 * 1024 * 1024

TM_FFN = 1024
FFN_SUB = 2
TM_MIX = 512
ML_CHUNK = MXU_DIM
GATE_PAD = LANES
NEG_BIG = -1e30


def _layer_norm(r, g, b):
    mu = jnp.mean(r, axis=-1, keepdims=True)
    d = r - mu
    var = jnp.mean(d * d, axis=-1, keepdims=True)
    return d * lax.rsqrt(var + LN_EPS) * g + b


def _sigmoid(x):
    return 1.0 / (1.0 + jnp.exp(-x))


def _dot(a, b):
    return jnp.dot(a, b, preferred_element_type=F32)


def _const_spec(shape):
    nd = len(shape)
    return pl.BlockSpec(shape, lambda *_: (0,) * nd)


def _ffn_ln_kernel(alpha, n_sub, x_ref, wg_ref, wu_ref, wd_ref, g_ref, b_ref, o_ref):
    ts = x_ref.shape[0] // n_sub
    pending = None
    for s in range(n_sub):
        rows = pl.ds(s * ts, ts)
        x = x_ref[rows, :]
        xb = x.astype(BF16)
        gate = _dot(xb, wg_ref[...])
        up = _dot(xb, wu_ref[...])
        if pending is not None:
            prows, r = pending
            o_ref[prows, :] = _layer_norm(r, g_ref[...], b_ref[...])
        h = (gate * _sigmoid(gate) * up).astype(BF16)
        y = _dot(h, wd_ref[...])
        pending = (rows, alpha * x + 0.5 * y)
    prows, r = pending
    o_ref[prows, :] = _layer_norm(r, g_ref[...], b_ref[...])


def _ffn_ln(x2, wg, wu, wd, g, b, alpha):
    n, d = x2.shape
    d_ff = wg.shape[1]
    tm = min(TM_FFN, n)
    return pl.pallas_call(
        functools.partial(_ffn_ln_kernel, alpha, FFN_SUB),
        grid=(n // tm,),
        in_specs=[
            pl.BlockSpec((tm, d), lambda i: (i, 0)),
            _const_spec((d, d_ff)),
            _const_spec((d, d_ff)),
            _const_spec((d_ff, d)),
            _const_spec((1, d)),
            _const_spec((1, d)),
        ],
        out_specs=pl.BlockSpec((tm, d), lambda i: (i, 0)),
        out_shape=jax.ShapeDtypeStruct((n, d), F32),
        compiler_params=pltpu.CompilerParams(
            dimension_semantics=("arbitrary",), vmem_limit_bytes=VMEM_LIMIT_BYTES),
    )(x2, wg, wu, wd, g, b)


def _mem_kv_kernel(scale, mem_ref, wkt_ref, wv_ref, kbd_ref, vbd_ref):
    memb = mem_ref[0].astype(BF16)
    n_mem = memb.shape[0]
    d_xa = wv_ref.shape[-1]
    hd = d_xa // XA_HEADS
    kt = lax.dot_general(wkt_ref[0], memb, (((1,), (1,)), ((), ())),
                         preferred_element_type=F32) * scale
    v = _dot(memb, wv_ref[0])
    kt4 = jnp.concatenate([kt] * XA_HEADS, axis=1)
    r = lax.broadcasted_iota(jnp.int32, kt4.shape, 0) // hd
    c = lax.broadcasted_iota(jnp.int32, kt4.shape, 1) // n_mem
    kbd_ref[0, 0] = jnp.where(r == c, kt4, 0.0).astype(BF16)
    v4 = jnp.concatenate([v] * XA_HEADS, axis=0)
    r = lax.broadcasted_iota(jnp.int32, v4.shape, 0) // n_mem
    c = lax.broadcasted_iota(jnp.int32, v4.shape, 1) // hd
    vbd_ref[0, 0] = jnp.where(r == c, v4, 0.0).astype(BF16)


def _mem_kv(mem, wkt, wv):
    b, n_mem, d = mem.shape
    depth, d_xa, _ = wkt.shape
    hd = d_xa // XA_HEADS
    return pl.pallas_call(
        functools.partial(_mem_kv_kernel, hd ** -0.5),
        grid=(depth, b),
        in_specs=[
            pl.BlockSpec((1, n_mem, d), lambda l, i: (i, 0, 0)),
            pl.BlockSpec((1, d_xa, d), lambda l, i: (l, 0, 0)),
            pl.BlockSpec((1, d, d_xa), lambda l, i: (l, 0, 0)),
        ],
        out_specs=[
            pl.BlockSpec((1, 1, d_xa, XA_HEADS * n_mem), lambda l, i: (l, i, 0, 0)),
            pl.BlockSpec((1, 1, XA_HEADS * n_mem, d_xa), lambda l, i: (l, i, 0, 0)),
        ],
        out_shape=[
            jax.ShapeDtypeStruct((depth, b, d_xa, XA_HEADS * n_mem), BF16),
            jax.ShapeDtypeStruct((depth, b, XA_HEADS * n_mem, d_xa), BF16),
        ],
        compiler_params=pltpu.CompilerParams(
            dimension_semantics=("arbitrary", "arbitrary"), vmem_limit_bytes=VMEM_LIMIT_BYTES),
    )(mem, wkt, wv)


def _cross_attention(q, kbd, vbd):
    tm, d_xa = q.shape
    n_mem = kbd.shape[1] // XA_HEADS
    hd = d_xa // XA_HEADS
    s = _dot(q.astype(BF16), kbd)
    ps, ls = [], []
    for h in range(XA_HEADS):
        sh = s[:, h * n_mem:(h + 1) * n_mem]
        p = jnp.exp(sh - jnp.max(sh, axis=-1, keepdims=True))
        ps.append(p.astype(BF16))
        ls.append(jnp.sum(p, axis=-1, keepdims=True))
    o = _dot(jnp.concatenate(ps, axis=1), vbd)
    lane_head = lax.broadcasted_iota(jnp.int32, (tm, d_xa), 1) // hd
    l = jnp.broadcast_to(ls[XA_HEADS - 1], (tm, d_xa))
    for h in range(XA_HEADS - 2, -1, -1):
        l = jnp.where(lane_head == h, ls[h], l)
    return o / l


def _causal_taps(buf_ref, cur, w_ref, tm):
    width = w_ref.shape[0]
    outs = []
    for c in range(buf_ref.shape[0]):
        lanes = slice(c * LANES, (c + 1) * LANES)
        cur_c = cur[:, lanes]
        buf_ref[c, pl.ds(SUBLANES, tm), :] = cur_c
        out = cur_c * w_ref[width - 1:width, lanes]
        for j in range(width - 1):
            off = SUBLANES - (width - 1) + j
            out = out + buf_ref[c, pl.ds(off, tm), :] * w_ref[j:j + 1, lanes]
        buf_ref[c, pl.ds(0, SUBLANES), :] = buf_ref[c, pl.ds(tm, SUBLANES), :]
        outs.append(out)
    return jnp.concatenate(outs, axis=1)


def _mix_conv_kernel(alpha, x_ref, win_ref, convw_ref, kbd_ref, vbd_ref, wout_ref, g_ref, b_ref,
                     o_ref, zbuf_ref):
    tm = x_ref.shape[0]
    d_tok = convw_ref.shape[1]

    @pl.when(pl.program_id(1) == 0)
    def _():
        zbuf_ref[:, pl.ds(0, SUBLANES), :] = jnp.zeros((zbuf_ref.shape[0], SUBLANES, LANES), F32)

    x = x_ref[...]
    u = _dot(x.astype(BF16), win_ref[...])
    b_gate = u[:, :d_tok]
    z = u[:, d_tok:2 * d_tok] * u[:, 2 * d_tok:3 * d_tok]
    tok = b_gate * _causal_taps(zbuf_ref, z, convw_ref, tm)
    xa = _cross_attention(u[:, 3 * d_tok:], kbd_ref[0, 0], vbd_ref[0, 0])
    cat = jnp.concatenate([tok.astype(BF16), xa.astype(BF16)], axis=1)
    mix = _dot(cat, wout_ref[...])
    o_ref[...] = _layer_norm(alpha * x + mix, g_ref[...], b_ref[...])


def _mix_conv(x2, batch, win, convw, kbd, vbd, layer, wout, g, b, alpha):
    n, d = x2.shape
    seq = n // batch
    tm = min(TM_MIX, seq)
    nt = seq // tm
    d_tok = convw.shape[1]
    kv_shape = kbd.shape
    return pl.pallas_call(
        functools.partial(_mix_conv_kernel, alpha),
        grid=(batch, nt),
        in_specs=[
            pl.BlockSpec((tm, d), lambda i, t: (i * nt + t, 0)),
            _const_spec(win.shape),
            _const_spec(convw.shape),
            pl.BlockSpec((1, 1) + kv_shape[2:], lambda i, t: (layer, i, 0, 0)),
            pl.BlockSpec((1, 1) + vbd.shape[2:], lambda i, t: (layer, i, 0, 0)),
            _const_spec(wout.shape),
            _const_spec((1, d)),
            _const_spec((1, d)),
        ],
        out_specs=pl.BlockSpec((tm, d), lambda i, t: (i * nt + t, 0)),
        out_shape=jax.ShapeDtypeStruct((n, d), F32),
        scratch_shapes=[pltpu.VMEM((d_tok // LANES, SUBLANES + tm, LANES), F32)],
        compiler_params=pltpu.CompilerParams(
            dimension_semantics=("arbitrary", "arbitrary"), vmem_limit_bytes=VMEM_LIMIT_BYTES),
    )(x2, win, convw, kbd, vbd, wout, g, b)


def _head_window(h, dh):
    pair_start = (h // 2) * 2 * dh
    if h % 2 == 0:
        return pair_start, 0, dh
    return pair_start + 2 * dh - MXU_DIM, MXU_DIM - dh, 0


def _mlstm_chunk_head(qw, kw, vaug, icol, bcol, irow, brow, b_last, tri, c_ref, m_ref, h, spare_lane):
    length, width = vaug.shape
    m = m_ref[h, 0:1, 0:1]
    log_d = jnp.where(tri, bcol - brow + irow, NEG_BIG)
    log_inter = bcol + m
    m_t = jnp.maximum(log_inter, jnp.max(log_d, axis=-1, keepdims=True))
    w_intra = jnp.exp(log_d - m_t)
    w_inter = jnp.exp(log_inter - m_t)
    s = lax.dot_general(qw, kw, (((1,), (1,)), ((), ())), preferred_element_type=F32)
    sc = (s * w_intra).astype(BF16)
    c_st = c_ref[h]
    num = _dot(sc, vaug.astype(BF16)) + w_inter * _dot(qw, c_st.astype(BF16))
    lane = lax.broadcasted_iota(jnp.int32, (length, width), 1)
    den = jnp.sum(jnp.where(lane == spare_lane, num, 0.0), axis=-1, keepdims=True)
    h_out = num / jnp.maximum(jnp.abs(den), jnp.exp(-m_t))
    log_w_row = b_last - brow + irow
    m_new = jnp.maximum(b_last + m, jnp.max(log_w_row, axis=-1, keepdims=True))
    w_k = jnp.exp(b_last - bcol + icol - m_new)
    decay = jnp.exp(b_last + m - m_new)
    upd = lax.dot_general(kw, (w_k * vaug).astype(BF16), (((0,), (0,)), ((), ())),
                          preferred_element_type=F32)
    c_ref[h] = decay * c_st + upd
    m_ref[h] = jnp.broadcast_to(m_new, m_ref.shape[1:])
    return h_out


def _mix_mlstm_kernel(alpha, x_ref, win_ref, convw_ref, bg_ref, hng_ref, kbd_ref, vbd_ref, wout_ref,
                      g_ref, b_ref, o_ref, cbuf_ref, c_ref, m_ref):
    tm = x_ref.shape[0]
    d_qk = convw_ref.shape[1]
    d_tok = d_qk // 2
    dh = d_tok // ML_HEADS
    length = ML_CHUNK
    width = MXU_DIM

    @pl.when(pl.program_id(1) == 0)
    def _():
        cbuf_ref[:, pl.ds(0, SUBLANES), :] = jnp.zeros((cbuf_ref.shape[0], SUBLANES, LANES), F32)
        c_ref[...] = jnp.zeros(c_ref.shape, F32)
        m_ref[...] = jnp.zeros(m_ref.shape, F32)

    x = x_ref[...]
    u = _dot(x.astype(BF16), win_ref[...])
    qk = _causal_taps(cbuf_ref, u[:, :d_qk], convw_ref, tm)
    qk = qk * _sigmoid(qk)
    q = qk[:, :d_tok].astype(BF16)
    k = qk[:, d_tok:] * (dh ** -0.5)
    v = u[:, d_qk:d_qk + d_tok]
    o_pre = u[:, d_qk + d_tok:d_qk + 2 * d_tok]
    gates = u[:, d_qk + 2 * d_tok:d_qk + 2 * d_tok + GATE_PAD] + bg_ref[...]
    q_mem = u[:, d_qk + 2 * d_tok + GATE_PAD:]

    glane = lax.broadcasted_iota(jnp.int32, gates.shape, 1)
    log_sig = jnp.minimum(gates, 0.0) - jnp.log(1.0 + jnp.exp(-jnp.abs(gates)))
    gl = jnp.where(glane < ML_HEADS, gates, log_sig)

    row = lax.broadcasted_iota(jnp.int32, (length, length), 0)
    col = lax.broadcasted_iota(jnp.int32, (length, length), 1)
    tri = row >= col
    tri_f = tri.astype(F32)
    wlane = lax.broadcasted_iota(jnp.int32, (length, width), 1)

    hn_rows = []
    for c in range(tm // length):
        rows = slice(c * length, (c + 1) * length)
        gc = gl[rows]
        cum = jnp.dot(tri_f, gc, preferred_element_type=F32,
                      precision=lax.Precision.HIGHEST)
        gct = gc.T
        cumt = cum.T
        wins = []
        for h in range(ML_HEADS):
            start, first, spare = _head_window(h, dh)
            valid = (wlane >= first) & (wlane < first + dh)
            qw = q[rows, start:start + width]
            kw = jnp.where(valid, k[rows, start:start + width], 0.0).astype(BF16)
            vaug = jnp.where(wlane == spare, 1.0, v[rows, start:start + width])
            icol = gc[:, h:h + 1]
            bcol = cum[:, ML_HEADS + h:ML_HEADS + h + 1]
            irow = gct[h:h + 1, :]
            brow = cumt[ML_HEADS + h:ML_HEADS + h + 1, :]
            b_last = cum[length - 1:length, ML_HEADS + h:ML_HEADS + h + 1]
            hw = _mlstm_chunk_head(qw, kw, vaug, icol, bcol, irow, brow, b_last, tri,
                                   c_ref, m_ref, h, spare)
            mu = jnp.sum(jnp.where(valid, hw, 0.0), axis=-1, keepdims=True) / dh
            d = jnp.where(valid, hw - mu, 0.0)
            var = jnp.sum(d * d, axis=-1, keepdims=True) / dh
            wins.append(d * lax.rsqrt(var + LN_EPS))
        pieces = []
        for p in range(ML_HEADS // 2):
            even, odd = wins[2 * p], wins[2 * p + 1]
            pieces += [even[:, :LANES], even[:, LANES:] + odd[:, :LANES], odd[:, LANES:]]
        hn_rows.append(jnp.concatenate(pieces, axis=1))
    hn = hn_rows[0] if len(hn_rows) == 1 else jnp.concatenate(hn_rows, axis=0)

    tok = _sigmoid(o_pre) * (hn * hng_ref[...])
    xa = _cross_attention(q_mem, kbd_ref[0, 0], vbd_ref[0, 0])
    cat = jnp.concatenate([tok.astype(BF16), xa.astype(BF16)], axis=1)
    mix = _dot(cat, wout_ref[...])
    o_ref[...] = _layer_norm(alpha * x + mix, g_ref[...], b_ref[...])


def _mix_mlstm(x2, batch, win, convw, bg, hng, kbd, vbd, layer, wout, g, b, alpha):
    n, d = x2.shape
    seq = n // batch
    tm = min(TM_MIX, seq)
    nt = seq // tm
    d_qk = convw.shape[1]
    return pl.pallas_call(
        functools.partial(_mix_mlstm_kernel, alpha),
        grid=(batch, nt),
        in_specs=[
            pl.BlockSpec((tm, d), lambda i, t: (i * nt + t, 0)),
            _const_spec(win.shape),
            _const_spec(convw.shape),
            _const_spec(bg.shape),
            _const_spec(hng.shape),
            pl.BlockSpec((1, 1) + kbd.shape[2:], lambda i, t: (layer, i, 0, 0)),
            pl.BlockSpec((1, 1) + vbd.shape[2:], lambda i, t: (layer, i, 0, 0)),
            _const_spec(wout.shape),
            _const_spec((1, d)),
            _const_spec((1, d)),
        ],
        out_specs=pl.BlockSpec((tm, d), lambda i, t: (i * nt + t, 0)),
        out_shape=jax.ShapeDtypeStruct((n, d), F32),
        scratch_shapes=[
            pltpu.VMEM((d_qk // LANES, SUBLANES + tm, LANES), F32),
            pltpu.VMEM((ML_HEADS, MXU_DIM, MXU_DIM), F32),
            pltpu.VMEM((ML_HEADS, SUBLANES, LANES), F32),
        ],
        compiler_params=pltpu.CompilerParams(
            dimension_semantics=("arbitrary", "arbitrary"), vmem_limit_bytes=VMEM_LIMIT_BYTES),
    )(x2, win, convw, bg, hng, kbd, vbd, wout, g, b)


def _prep_mlstm(w, b_gates, head_norm_g, d_tok):
    n_gate = 2 * ML_HEADS
    gate_w = jnp.pad(w[:, 4 * d_tok:4 * d_tok + n_gate], ((0, 0), (0, GATE_PAD - n_gate)))
    win = jnp.concatenate([w[:, :4 * d_tok], gate_w, w[:, 4 * d_tok + n_gate:]], axis=1).astype(BF16)
    bg = jnp.pad(b_gates, (0, GATE_PAD - n_gate))[None, :]
    return win, bg, head_norm_g.reshape(1, d_tok)


def kernel(x, mem, ln_g, ln_b, ffn_w_gate, ffn_w_up, ffn_w_down, w_kv_mem, w_out, w_in_conv, conv_w,
           w_in_mlstm, b_gates, qk_conv_w, head_norm_g):
    batch, seq, d = x.shape
    depth = ffn_w_gate.shape[0]
    d_xa = w_kv_mem.shape[-1] // 2
    d_tok = conv_w.shape[-1]
    n_mixers = 2
    alpha = (2.0 * depth) ** 0.25
    assert seq % ML_CHUNK == 0 and d_tok % ML_HEADS == 0
    assert 2 * (d_tok // ML_HEADS) % LANES == 0 and d_tok // ML_HEADS <= MXU_DIM

    wg = ffn_w_gate.astype(BF16)
    wu = ffn_w_up.astype(BF16)
    wd = ffn_w_down.astype(BF16)
    wo = w_out.astype(BF16)
    wkt = jnp.swapaxes(w_kv_mem[:, :, :d_xa], 1, 2).astype(BF16)
    wv = w_kv_mem[:, :, d_xa:].astype(BF16)
    kbd, vbd = _mem_kv(mem, wkt, wv)

    x2 = x.reshape(batch * seq, d)
    for l in range(depth):
        g = ln_g[l][:, None, :]
        b = ln_b[l][:, None, :]
        x2 = _ffn_ln(x2, wg[l, 0], wu[l, 0], wd[l, 0], g[0], b[0], alpha)
        j = l // n_mixers
        if l % n_mixers == 0:
            x2 = _mix_conv(x2, batch, w_in_conv[j].astype(BF16), conv_w[j], kbd, vbd, l, wo[l],
                           g[1], b[1], alpha)
        else:
            win, bg, hng = _prep_mlstm(w_in_mlstm[j], b_gates[j], head_norm_g[j], d_tok)
            x2 = _mix_mlstm(x2, batch, win, qk_conv_w[j], bg, hng, kbd, vbd, l, wo[l],
                            g[1], b[1], alpha)
        x2 = _ffn_ln(x2, wg[l, 1], wu[l, 1], wd[l, 1], g[2], b[2], alpha)
    return x2.reshape(batch, seq, d)
```

```python
import functools

import jax
import jax.numpy as jnp
from jax import lax
from jax.experimental import pallas as pl
from jax.experimental.pallas import tpu as pltpu

F32 = jnp.float32
BF16 = jnp.bfloat16

XA_HEADS = 4
CONV_WIDTH = 3
QK_CONV_WIDTH = 4
ML_HEADS = 4
LN_EPS = 1e-5

LANES = 128
SUBLANES = 8
MXU_DIM = 256
VMEM_LIMIT_BYTES = 56 * 1024 * 1024

TM_FFN = 1024
FFN_SUB = 2
TM_MIX = 512
PROJ_CHUNK = 256
ML_CHUNK = MXU_DIM
GATE_PAD = LANES
NEG_BIG = -1e30


def _layer_norm(r, g, b):
    mu = jnp.mean(r, axis=-1, keepdims=True)
    d = r - mu
    var = jnp.mean(d * d, axis=-1, keepdims=True)
    return d * lax.rsqrt(var + LN_EPS) * g + b


def _sigmoid(x):
    return 1.0 / (1.0 + jnp.exp(-x))


def _dot(a, b):
    return jnp.dot(a, b, preferred_element_type=F32)


def _dot_nt(a, b):
    return lax.dot_general(a, b, (((1,), (1,)), ((), ())), preferred_element_type=F32)


def _dot_tn(a, b):
    return lax.dot_general(a, b, (((0,), (0,)), ((), ())), preferred_element_type=F32)


def _const_spec(shape):
    nd = len(shape)
    return pl.BlockSpec(shape, lambda *_: (0,) * nd)


def _ffn_ln_kernel(alpha, n_sub, x_ref, wg_ref, wu_ref, wd_ref, g_ref, b_ref, o_ref):
    ts = x_ref.shape[0] // n_sub
    pending = None
    for s in range(n_sub):
        rows = pl.ds(s * ts, ts)
        x = x_ref[rows, :]
        xb = x.astype(BF16)
        gate = _dot(xb, wg_ref[...])
        up = _dot(xb, wu_ref[...])
        if pending is not None:
            prows, r = pending
            o_ref[prows, :] = _layer_norm(r, g_ref[...], b_ref[...])
        h = (gate * _sigmoid(gate) * up).astype(BF16)
        y = _dot(h, wd_ref[...])
        pending = (rows, alpha * x + 0.5 * y)
    prows, r = pending
    o_ref[prows, :] = _layer_norm(r, g_ref[...], b_ref[...])


def _ffn_ln(x2, wg, wu, wd, g, b, alpha):
    n, d = x2.shape
    d_ff = wg.shape[1]
    tm = min(TM_FFN, n)
    return pl.pallas_call(
        functools.partial(_ffn_ln_kernel, alpha, FFN_SUB),
        grid=(n // tm,),
        in_specs=[
            pl.BlockSpec((tm, d), lambda i: (i, 0)),
            _const_spec((d, d_ff)),
            _const_spec((d, d_ff)),
            _const_spec((d_ff, d)),
            _const_spec((1, d)),
            _const_spec((1, d)),
        ],
        out_specs=pl.BlockSpec((tm, d), lambda i: (i, 0)),
        out_shape=jax.ShapeDtypeStruct((n, d), F32),
        compiler_params=pltpu.CompilerParams(
            dimension_semantics=("arbitrary",), vmem_limit_bytes=VMEM_LIMIT_BYTES),
    )(x2, wg, wu, wd, g, b)


def _mem_kv_kernel(scale, mem_ref, wkt_ref, wv_ref, kbd_ref, vbd_ref):
    memb = mem_ref[0].astype(BF16)
    n_mem = memb.shape[0]
    d_xa = wv_ref.shape[-1]
    hd = d_xa // XA_HEADS
    kt = _dot_nt(wkt_ref[0], memb) * scale
    v = _dot(memb, wv_ref[0])
    kt4 = jnp.concatenate([kt] * XA_HEADS, axis=1)
    r = lax.broadcasted_iota(jnp.int32, kt4.shape, 0) // hd
    c = lax.broadcasted_iota(jnp.int32, kt4.shape, 1) // n_mem
    kbd_ref[0, 0] = jnp.where(r == c, kt4, 0.0).astype(BF16)
    v4 = jnp.concatenate([v] * XA_HEADS, axis=0)
    r = lax.broadcasted_iota(jnp.int32, v4.shape, 0) // n_mem
    c = lax.broadcasted_iota(jnp.int32, v4.shape, 1) // hd
    vbd_ref[0, 0] = jnp.where(r == c, v4, 0.0).astype(BF16)


def _mem_kv(mem, wkt, wv):
    b, n_mem, d = mem.shape
    depth, d_xa, _ = wkt.shape
    hd = d_xa // XA_HEADS
    return pl.pallas_call(
        functools.partial(_mem_kv_kernel, hd ** -0.5),
        grid=(depth, b),
        in_specs=[
            pl.BlockSpec((1, n_mem, d), lambda l, i: (i, 0, 0)),
            pl.BlockSpec((1, d_xa, d), lambda l, i: (l, 0, 0)),
            pl.BlockSpec((1, d, d_xa), lambda l, i: (l, 0, 0)),
        ],
        out_specs=[
            pl.BlockSpec((1, 1, d_xa, XA_HEADS * n_mem), lambda l, i: (l, i, 0, 0)),
            pl.BlockSpec((1, 1, XA_HEADS * n_mem, d_xa), lambda l, i: (l, i, 0, 0)),
        ],
        out_shape=[
            jax.ShapeDtypeStruct((depth, b, d_xa, XA_HEADS * n_mem), BF16),
            jax.ShapeDtypeStruct((depth, b, XA_HEADS * n_mem, d_xa), BF16),
        ],
        compiler_params=pltpu.CompilerParams(
            dimension_semantics=("arbitrary", "arbitrary"), vmem_limit_bytes=VMEM_LIMIT_BYTES),
    )(mem, wkt, wv)


def _cross_attention(q, kbd, vbd):
    tm, d_xa = q.shape
    n_mem = kbd.shape[1] // XA_HEADS
    hd = d_xa // XA_HEADS
    s = _dot(q.astype(BF16), kbd)
    heads = range(XA_HEADS)
    sh = [s[:, h * n_mem:(h + 1) * n_mem] for h in heads]
    mx = [jnp.max(sh[h], axis=-1, keepdims=True) for h in heads]
    p = [jnp.exp(sh[h] - mx[h]) for h in heads]
    ls = [jnp.sum(p[h], axis=-1, keepdims=True) for h in heads]
    o = _dot(jnp.concatenate([p[h].astype(BF16) for h in heads], axis=1), vbd)
    lane_head = lax.broadcasted_iota(jnp.int32, (tm, d_xa), 1) // hd
    l = jnp.broadcast_to(ls[XA_HEADS - 1], (tm, d_xa))
    for h in range(XA_HEADS - 2, -1, -1):
        l = jnp.where(lane_head == h, ls[h], l)
    return o / l


def _skewed_body(nt, x_ref, win_ref, u_a_ref, u_b_ref, reset_fn, make_phases):
    i = pl.program_id(0)

    @pl.when(i == 0)
    def _():
        u_b_ref[...] = jnp.zeros(u_b_ref.shape, F32)

    @pl.when(lax.rem(jnp.maximum(i - 1, 0), nt) == 0)
    def _():
        reset_fn()

    width = win_ref.shape[1]
    chunks = [(lo, min(lo + PROJ_CHUNK, width)) for lo in range(0, width, PROJ_CHUNK)]

    def step(u_read_ref, u_write_ref):
        xb = x_ref[...].astype(BF16)
        phases = make_phases(u_read_ref)
        total = sum(w for _, w in phases)
        done, acc = 0, 0
        for phase, weight in phases:
            acc += weight
            upto = -(-acc * len(chunks) // total)
            for lo, hi in chunks[done:upto]:
                u_write_ref[:, lo:hi] = _dot(xb, win_ref[:, lo:hi])
            done = max(done, upto)
            phase()

    parity = lax.rem(i, 2)

    @pl.when(parity == 0)
    def _():
        step(u_b_ref, u_a_ref)

    @pl.when(parity == 1)
    def _():
        step(u_a_ref, u_b_ref)


def _skewed_mixer_call(body, x2, batch, layer, kbd, vbd, operands, u_width, extra_scratch):
    n, d = x2.shape
    seq = n // batch
    tm = min(TM_MIX, seq)
    nt = seq // tm
    n_tiles = n // tm

    def prev(i):
        return jnp.maximum(i - 1, 0)

    in_specs = [
        pl.BlockSpec((tm, d), lambda i: (jnp.minimum(i, n_tiles - 1), 0)),
        pl.BlockSpec((tm, d), lambda i: (prev(i), 0)),
        pl.BlockSpec((1, 1) + kbd.shape[2:], lambda i: (layer, prev(i) // nt, 0, 0)),
        pl.BlockSpec((1, 1) + vbd.shape[2:], lambda i: (layer, prev(i) // nt, 0, 0)),
    ] + [_const_spec(a.shape) for a in operands]
    return pl.pallas_call(
        functools.partial(body, nt),
        grid=(n_tiles + 1,),
        in_specs=in_specs,
        out_specs=pl.BlockSpec((tm, d), lambda i: (prev(i), 0)),
        out_shape=jax.ShapeDtypeStruct((n, d), F32),
        scratch_shapes=[pltpu.VMEM((tm, u_width), F32), pltpu.VMEM((tm, u_width), F32)] + extra_scratch,
        compiler_params=pltpu.CompilerParams(
            dimension_semantics=("arbitrary",), vmem_limit_bytes=VMEM_LIMIT_BYTES),
    )(x2, x2, kbd, vbd, *operands)


def _causal_taps(buf_ref, cur_tile, w_ref, tm, tiles):
    width = w_ref.shape[0]
    outs = []
    for c in tiles:
        lanes = slice(c * LANES, (c + 1) * LANES)
        cur_c = cur_tile(c)
        buf_ref[c, pl.ds(SUBLANES, tm), :] = cur_c
        out = cur_c * w_ref[width - 1:width, lanes]
        for j in range(width - 1):
            off = SUBLANES - (width - 1) + j
            out = out + buf_ref[c, pl.ds(off, tm), :] * w_ref[j:j + 1, lanes]
        buf_ref[c, pl.ds(0, SUBLANES), :] = buf_ref[c, pl.ds(tm, SUBLANES), :]
        outs.append(out)
    return jnp.concatenate(outs, axis=1)


def _mix_conv_kernel(alpha, nt, x_ref, xp_ref, kbd_ref, vbd_ref, win_ref, convw_ref, wout_ref, g_ref,
                     b_ref, o_ref, u_ref, u_next_ref, zbuf_ref):
    tm = x_ref.shape[0]
    d_tok = convw_ref.shape[1]

    def reset():
        zbuf_ref[:, pl.ds(0, SUBLANES), :] = jnp.zeros((zbuf_ref.shape[0], SUBLANES, LANES), F32)

    def make_phases(u_ref):
        st = {}

        def z_tile(c):
            lo = d_tok + c * LANES
            return u_ref[:, lo:lo + LANES] * u_ref[:, d_tok + lo:d_tok + lo + LANES]

        def conv():
            conv = _causal_taps(zbuf_ref, z_tile, convw_ref, tm, range(d_tok // LANES))
            st["tok"] = (u_ref[:, :d_tok] * conv).astype(BF16)

        def attention():
            st["xa"] = _cross_attention(u_ref[:, 3 * d_tok:], kbd_ref[0, 0], vbd_ref[0, 0]).astype(BF16)

        def out_proj():
            st["mix"] = _dot(jnp.concatenate([st["tok"], st["xa"]], axis=1), wout_ref[...])

        def norm():
            o_ref[...] = _layer_norm(alpha * xp_ref[...] + st["mix"], g_ref[...], b_ref[...])

        return [(conv, 2), (attention, 1), (out_proj, 0), (norm, 2)]

    _skewed_body(nt, x_ref, win_ref, u_ref, u_next_ref, reset, make_phases)


def _mix_conv(x2, batch, win, convw, kbd, vbd, layer, wout, g, b, alpha):
    d_tok = convw.shape[1]
    tm = min(TM_MIX, x2.shape[0] // batch)
    return _skewed_mixer_call(
        functools.partial(_mix_conv_kernel, alpha), x2, batch, layer, kbd, vbd,
        [win, convw, wout, g, b], win.shape[1],
        [pltpu.VMEM((d_tok // LANES, SUBLANES + tm, LANES), F32)])


def _head_window(h, dh):
    pair_start = (h // 2) * 2 * dh
    if h % 2 == 0:
        return pair_start, 0, dh
    return pair_start + 2 * dh - MXU_DIM, MXU_DIM - dh, 0


def _mix_mlstm_kernel(alpha, nt, x_ref, xp_ref, kbd_ref, vbd_ref, win_ref, convw_ref, bg_ref, hng_ref,
                      wout_ref, g_ref, b_ref, o_ref, u_ref, u_next_ref, cbuf_ref, c_ref, m_ref):
    tm = x_ref.shape[0]
    d_qk = convw_ref.shape[1]
    d_tok = d_qk // 2
    dh = d_tok // ML_HEADS
    length = ML_CHUNK
    width = MXU_DIM
    n_chunks = tm // length
    qk_tiles = d_qk // LANES
    heads = range(ML_HEADS)
    pairs = [(c, h) for c in range(n_chunks) for h in heads]

    def reset():
        cbuf_ref[:, pl.ds(0, SUBLANES), :] = jnp.zeros((cbuf_ref.shape[0], SUBLANES, LANES), F32)
        c_ref[...] = jnp.zeros(c_ref.shape, F32)
        m_ref[...] = jnp.zeros(m_ref.shape, F32)

    def make_phases(u_ref):
        st = {}
        window = [_head_window(h, dh) for h in heads]

        def rows(c):
            return slice(c * length, (c + 1) * length)

        def u_tile(c):
            return u_ref[:, c * LANES:(c + 1) * LANES]

        def conv(part):
            def phase():
                lo = part * qk_tiles // 4
                a = _causal_taps(cbuf_ref, u_tile, convw_ref, tm, range(lo, lo + qk_tiles // 4))
                a = a * _sigmoid(a)
                st["qk", part] = a.astype(BF16) if part < 2 else a * (dh ** -0.5)
                if part == 1:
                    st["q"] = jnp.concatenate([st.pop(("qk", 0)), st.pop(("qk", 1))], axis=1)
                if part == 3:
                    st["k"] = jnp.concatenate([st.pop(("qk", 2)), st.pop(("qk", 3))], axis=1)
            return phase

        def gate_math():
            gates = u_ref[:, d_qk + 2 * d_tok:d_qk + 2 * d_tok + GATE_PAD] + bg_ref[...]
            glane = lax.broadcasted_iota(jnp.int32, gates.shape, 1)
            log_sig = jnp.minimum(gates, 0.0) - jnp.log(1.0 + jnp.exp(-jnp.abs(gates)))
            gl = jnp.where(glane < ML_HEADS, gates, log_sig)
            row = lax.broadcasted_iota(jnp.int32, (length, length), 0)
            col = lax.broadcasted_iota(jnp.int32, (length, length), 1)
            st["tri"] = row >= col
            tri_f = st["tri"].astype(F32)
            wlane = lax.broadcasted_iota(jnp.int32, (length, width), 1)
            st["valid"] = [(wlane >= first) & (wlane < first + dh) for _, first, _ in window]
            st["spare"] = [wlane == spare for _, _, spare in window]
            for c in range(n_chunks):
                gc = gl[rows(c)]
                cum = jnp.dot(tri_f, gc, preferred_element_type=F32,
                              precision=lax.Precision.HIGHEST)
                gct, cumt = gc.T, cum.T
                for h in heads:
                    st["icol", c, h] = gc[:, h:h + 1]
                    st["bcol", c, h] = cum[:, ML_HEADS + h:ML_HEADS + h + 1]
                    st["irow", c, h] = gct[h:h + 1, :]
                    st["brow", c, h] = cumt[ML_HEADS + h:ML_HEADS + h + 1, :]
                    st["blast", c, h] = cum[length - 1:length, ML_HEADS + h:ML_HEADS + h + 1]

        def decay_weights():
            for c, h in pairs:
                st["logd", c, h] = jnp.where(
                    st["tri"], st["bcol", c, h] - st["brow", c, h] + st["irow", c, h], NEG_BIG)
            for c, h in pairs:
                st["a", c, h] = jnp.max(st["logd", c, h], axis=-1, keepdims=True)
            for c, h in pairs:
                st["p", c, h] = jnp.exp(st.pop(("logd", c, h)) - st["a", c, h])

        def scores():
            for c, h in pairs:
                start = window[h][0]
                st["qw", c, h] = st["q"][rows(c), start:start + width]
                st["kw", c, h] = jnp.where(st["valid"][h], st["k"][rows(c), start:start + width],
                                           0.0).astype(BF16)
            for c, h in pairs:
                s = _dot_nt(st["qw", c, h], st["kw", c, h])
                st["sc", c, h] = (s * st.pop(("p", c, h))).astype(BF16)

        def intra():
            for c, h in pairs:
                start = window[h][0]
                vw = u_ref[rows(c), d_qk + start:d_qk + start + width]
                st["vaug", c, h] = jnp.where(st["spare"][h], 1.0, vw)
            for c, h in pairs:
                st["intra", c, h] = _dot(st.pop(("sc", c, h)), st["vaug", c, h].astype(BF16))

        def recur(c):
            def combine():
                m = [m_ref[h, 0:1, 0:1] for h in heads]
                c_st = [c_ref[h] for h in heads]
                inter = [_dot(st["qw", c, h], c_st[h].astype(BF16)) for h in heads]
                log_inter = [st["bcol", c, h] + m[h] for h in heads]
                m_t = [jnp.maximum(log_inter[h], st["a", c, h]) for h in heads]
                num = [jnp.exp(st["a", c, h] - m_t[h]) * st.pop(("intra", c, h))
                       + jnp.exp(log_inter[h] - m_t[h]) * inter[h] for h in heads]
                den = [jnp.sum(jnp.where(st["spare"][h], num[h], 0.0), axis=-1, keepdims=True)
                       for h in heads]
                for h in heads:
                    st["m", h], st["c_st", h] = m[h], c_st[h]
                    st["hw", h] = num[h] / jnp.maximum(jnp.abs(den[h]), jnp.exp(-m_t[h]))

            def update():
                m = [st.pop(("m", h)) for h in heads]
                blast = [st["blast", c, h] for h in heads]
                log_w_row = [blast[h] - st["brow", c, h] + st["irow", c, h] for h in heads]
                m_new = [jnp.maximum(blast[h] + m[h], jnp.max(log_w_row[h], axis=-1, keepdims=True))
                         for h in heads]
                w_k = [jnp.exp(blast[h] - st["bcol", c, h] + st["icol", c, h] - m_new[h])
                       for h in heads]
                upd = [_dot_tn(st["kw", c, h], (w_k[h] * st.pop(("vaug", c, h))).astype(BF16))
                       for h in heads]
                for h in heads:
                    c_ref[h] = jnp.exp(blast[h] + m[h] - m_new[h]) * st.pop(("c_st", h)) + upd[h]
                    m_ref[h] = jnp.broadcast_to(m_new[h], m_ref.shape[1:])

            def head_norm():
                hw = [st.pop(("hw", h)) for h in heads]
                mu = [jnp.sum(jnp.where(st["valid"][h], hw[h], 0.0), axis=-1, keepdims=True) / dh
                      for h in heads]
                d = [jnp.where(st["valid"][h], hw[h] - mu[h], 0.0) for h in heads]
                var = [jnp.sum(d[h] * d[h], axis=-1, keepdims=True) / dh for h in heads]
                for h in heads:
                    st["win", c, h] = d[h] * lax.rsqrt(var[h] + LN_EPS)

            return [(combine, 1), (update, 0), (head_norm, 1)]

        def tail_gate():
            hn_rows = []
            for c in range(n_chunks):
                pieces = []
                for p in range(ML_HEADS // 2):
                    even, odd = st["win", c, 2 * p], st["win", c, 2 * p + 1]
                    pieces += [even[:, :LANES], even[:, LANES:] + odd[:, :LANES], odd[:, LANES:]]
                hn_rows.append(jnp.concatenate(pieces, axis=1))
            hn = hn_rows[0] if n_chunks == 1 else jnp.concatenate(hn_rows, axis=0)
            o_pre = u_ref[:, d_qk + d_tok:d_qk + 2 * d_tok]
            st["tok"] = (_sigmoid(o_pre) * (hn * hng_ref[...])).astype(BF16)

        def tail_attention():
            q_mem = u_ref[:, d_qk + 2 * d_tok + GATE_PAD:]
            xa = _cross_attention(q_mem, kbd_ref[0, 0], vbd_ref[0, 0])
            cat = jnp.concatenate([st["tok"], xa.astype(BF16)], axis=1)
            st["mix"] = _dot(cat, wout_ref[...])

        def norm():
            o_ref[...] = _layer_norm(alpha * xp_ref[...] + st["mix"], g_ref[...], b_ref[...])

        recurs = [p for c in range(n_chunks) for p in recur(c)]
        return ([(conv(part), 1) for part in range(4)]
                + [(gate_math, 1), (decay_weights, 1), (scores, 1), (intra, 0)]
                + recurs + [(tail_gate, 1), (tail_attention, 1), (norm, 1)])

    _skewed_body(nt, x_ref, win_ref, u_ref, u_next_ref, reset, make_phases)


def _mix_mlstm(x2, batch, win, convw, bg, hng, kbd, vbd, layer, wout, g, b, alpha):
    d_qk = convw.shape[1]
    tm = min(TM_MIX, x2.shape[0] // batch)
    return _skewed_mixer_call(
        functools.partial(_mix_mlstm_kernel, alpha), x2, batch, layer, kbd, vbd,
        [win, convw, bg, hng, wout, g, b], win.shape[1],
        [pltpu.VMEM((d_qk // LANES, SUBLANES + tm, LANES), F32),
         pltpu.VMEM((ML_HEADS, MXU_DIM, MXU_DIM), F32),
         pltpu.VMEM((ML_HEADS, SUBLANES, LANES), F32)])


def _prep_mlstm(w, b_gates, head_norm_g, d_tok):
    n_gate = 2 * ML_HEADS
    gate_w = jnp.pad(w[:, 4 * d_tok:4 * d_tok + n_gate], ((0, 0), (0, GATE_PAD - n_gate)))
    win = jnp.concatenate([w[:, :4 * d_tok], gate_w, w[:, 4 * d_tok + n_gate:]], axis=1).astype(BF16)
    bg = jnp.pad(b_gates, (0, GATE_PAD - n_gate))[None, :]
    return win, bg, head_norm_g.reshape(1, d_tok)


def kernel(x, mem, ln_g, ln_b, ffn_w_gate, ffn_w_up, ffn_w_down, w_kv_mem, w_out, w_in_conv, conv_w,
           w_in_mlstm, b_gates, qk_conv_w, head_norm_g):
    batch, seq, d = x.shape
    depth = ffn_w_gate.shape[0]
    d_xa = w_kv_mem.shape[-1] // 2
    d_tok = conv_w.shape[-1]
    n_mixers = 2
    alpha = (2.0 * depth) ** 0.25
    assert seq % ML_CHUNK == 0 and d_tok % ML_HEADS == 0
    assert 2 * (d_tok // ML_HEADS) % LANES == 0 and d_tok // ML_HEADS <= MXU_DIM

    wg = ffn_w_gate.astype(BF16)
    wu = ffn_w_up.astype(BF16)
    wd = ffn_w_down.astype(BF16)
    wo = w_out.astype(BF16)
    wkt = jnp.swapaxes(w_kv_mem[:, :, :d_xa], 1, 2).astype(BF16)
    wv = w_kv_mem[:, :, d_xa:].astype(BF16)
    kbd, vbd = _mem_kv(mem, wkt, wv)

    x2 = x.reshape(batch * seq, d)
    for l in range(depth):
        g = ln_g[l][:, None, :]
        b = ln_b[l][:, None, :]
        x2 = _ffn_ln(x2, wg[l, 0], wu[l, 0], wd[l, 0], g[0], b[0], alpha)
        j = l // n_mixers
        if l % n_mixers == 0:
            x2 = _mix_conv(x2, batch, w_in_conv[j].astype(BF16), conv_w[j], kbd, vbd, l, wo[l],
                           g[1], b[1], alpha)
        else:
            win, bg, hng = _prep_mlstm(w_in_mlstm[j], b_gates[j], head_norm_g[j], d_tok)
            x2 = _mix_mlstm(x2, batch, win, qk_conv_w[j], bg, hng, kbd, vbd, l, wo[l],
                            g[1], b[1], alpha)
        x2 = _ffn_ln(x2, wg[l, 1], wu[l, 1], wd[l, 1], g[2], b[2], alpha)
    return x2.reshape(batch, seq, d)
```

```python
import functools

import jax
import jax.numpy as jnp
from jax import lax
from jax.experimental import pallas as pl
from jax.experimental.pallas import tpu as pltpu

F32 = jnp.float32
BF16 = jnp.bfloat16

XA_HEADS = 4
CONV_WIDTH = 3
QK_CONV_WIDTH = 4
ML_HEADS = 4
LN_EPS = 1e-5

LANES = 128
SUBLANES = 8
MXU_DIM = 256
VMEM_LIMIT_BYTES = 56 * 1024 * 1024

TM_FFN = 1024
FFN_SUB = 2
TM_MIX = 512
PROJ_CHUNK = 256
CONV_PHASE_WEIGHTS = (1, 3, 2, 4)
MLSTM_PHASE_WEIGHTS = (1, 1, 1, 1, 1, 1, 1, 0, 1, 0, 1, 1, 1, 1)
ML_CHUNK = MXU_DIM
GATE_PAD = LANES
NEG_BIG = -1e30


def _layer_norm(r, g, b):
    mu = jnp.mean(r, axis=-1, keepdims=True)
    d = r - mu
    var = jnp.mean(d * d, axis=-1, keepdims=True)
    return d * lax.rsqrt(var + LN_EPS) * g + b


def _sigmoid(x):
    return 1.0 / (1.0 + jnp.exp(-x))


def _dot(a, b):
    return jnp.dot(a, b, preferred_element_type=F32)


def _dot_nt(a, b):
    return lax.dot_general(a, b, (((1,), (1,)), ((), ())), preferred_element_type=F32)


def _dot_tn(a, b):
    return lax.dot_general(a, b, (((0,), (0,)), ((), ())), preferred_element_type=F32)


def _const_spec(shape):
    nd = len(shape)
    return pl.BlockSpec(shape, lambda *_: (0,) * nd)


def _ffn_ln_kernel(alpha, n_sub, x_ref, wg_ref, wu_ref, wd_ref, g_ref, b_ref, o_ref):
    ts = x_ref.shape[0] // n_sub
    pending = None
    for s in range(n_sub):
        rows = pl.ds(s * ts, ts)
        x = x_ref[rows, :]
        xb = x.astype(BF16)
        gate = _dot(xb, wg_ref[...])
        up = _dot(xb, wu_ref[...])
        if pending is not None:
            prows, r = pending
            o_ref[prows, :] = _layer_norm(r, g_ref[...], b_ref[...])
        h = (gate * _sigmoid(gate) * up).astype(BF16)
        y = _dot(h, wd_ref[...])
        pending = (rows, alpha * x + 0.5 * y)
    prows, r = pending
    o_ref[prows, :] = _layer_norm(r, g_ref[...], b_ref[...])


def _ffn_ln(x2, wg, wu, wd, layer, pos, g, b, alpha):
    n, d = x2.shape
    d_ff = wg.shape[-1]
    tm = min(TM_FFN, n)
    return pl.pallas_call(
        functools.partial(_ffn_ln_kernel, alpha, FFN_SUB),
        grid=(n // tm,),
        in_specs=[
            pl.BlockSpec((tm, d), lambda i: (i, 0)),
            pl.BlockSpec((None, None, d, d_ff), lambda i: (layer, pos, 0, 0)),
            pl.BlockSpec((None, None, d, d_ff), lambda i: (layer, pos, 0, 0)),
            pl.BlockSpec((None, None, d_ff, d), lambda i: (layer, pos, 0, 0)),
            _const_spec((1, d)),
            _const_spec((1, d)),
        ],
        out_specs=pl.BlockSpec((tm, d), lambda i: (i, 0)),
        out_shape=jax.ShapeDtypeStruct((n, d), F32),
        compiler_params=pltpu.CompilerParams(
            dimension_semantics=("arbitrary",), vmem_limit_bytes=VMEM_LIMIT_BYTES),
    )(x2, wg, wu, wd, g, b)


def _mem_kv_kernel(scale, mem_ref, wkt_ref, wv_ref, kbd_ref, vbd_ref):
    memb = mem_ref[0].astype(BF16)
    n_mem = memb.shape[0]
    d_xa = wv_ref.shape[-1]
    hd = d_xa // XA_HEADS
    kt = _dot_nt(wkt_ref[0], memb) * scale
    v = _dot(memb, wv_ref[0])
    kt4 = jnp.concatenate([kt] * XA_HEADS, axis=1)
    r = lax.broadcasted_iota(jnp.int32, kt4.shape, 0) // hd
    c = lax.broadcasted_iota(jnp.int32, kt4.shape, 1) // n_mem
    kbd_ref[0, 0] = jnp.where(r == c, kt4, 0.0).astype(BF16)
    v4 = jnp.concatenate([v] * XA_HEADS, axis=0)
    r = lax.broadcasted_iota(jnp.int32, v4.shape, 0) // n_mem
    c = lax.broadcasted_iota(jnp.int32, v4.shape, 1) // hd
    vbd_ref[0, 0] = jnp.where(r == c, v4, 0.0).astype(BF16)


def _mem_kv(mem, wkt, wv):
    b, n_mem, d = mem.shape
    depth, d_xa, _ = wkt.shape
    hd = d_xa // XA_HEADS
    return pl.pallas_call(
        functools.partial(_mem_kv_kernel, hd ** -0.5),
        grid=(depth, b),
        in_specs=[
            pl.BlockSpec((1, n_mem, d), lambda l, i: (i, 0, 0)),
            pl.BlockSpec((1, d_xa, d), lambda l, i: (l, 0, 0)),
            pl.BlockSpec((1, d, d_xa), lambda l, i: (l, 0, 0)),
        ],
        out_specs=[
            pl.BlockSpec((1, 1, d_xa, XA_HEADS * n_mem), lambda l, i: (l, i, 0, 0)),
            pl.BlockSpec((1, 1, XA_HEADS * n_mem, d_xa), lambda l, i: (l, i, 0, 0)),
        ],
        out_shape=[
            jax.ShapeDtypeStruct((depth, b, d_xa, XA_HEADS * n_mem), BF16),
            jax.ShapeDtypeStruct((depth, b, XA_HEADS * n_mem, d_xa), BF16),
        ],
        compiler_params=pltpu.CompilerParams(
            dimension_semantics=("arbitrary", "arbitrary"), vmem_limit_bytes=VMEM_LIMIT_BYTES),
    )(mem, wkt, wv)


def _cross_attention(q, kbd, vbd):
    tm, d_xa = q.shape
    n_mem = kbd.shape[1] // XA_HEADS
    hd = d_xa // XA_HEADS
    s = _dot(q.astype(BF16), kbd)
    heads = range(XA_HEADS)
    sh = [s[:, h * n_mem:(h + 1) * n_mem] for h in heads]
    mx = [jnp.max(sh[h], axis=-1, keepdims=True) for h in heads]
    p = [jnp.exp(sh[h] - mx[h]) for h in heads]
    ls = [jnp.sum(p[h], axis=-1, keepdims=True) for h in heads]
    o = _dot(jnp.concatenate([p[h].astype(BF16) for h in heads], axis=1), vbd)
    lane_head = lax.broadcasted_iota(jnp.int32, (tm, d_xa), 1) // hd
    l = jnp.broadcast_to(ls[XA_HEADS - 1], (tm, d_xa))
    for h in range(XA_HEADS - 2, -1, -1):
        l = jnp.where(lane_head == h, ls[h], l)
    return o / l


def _skewed_body(nt, x_ref, win_ref, u_a_ref, u_b_ref, reset_fn, make_phases, proj_first):
    i = pl.program_id(0)

    @pl.when(i == 0)
    def _():
        u_b_ref[...] = jnp.zeros(u_b_ref.shape, F32)

    @pl.when(lax.rem(jnp.maximum(i - 1, 0), nt) == 0)
    def _():
        reset_fn()

    width = win_ref.shape[1]
    chunks = [(lo, min(lo + PROJ_CHUNK, width)) for lo in range(0, width, PROJ_CHUNK)]

    def step(u_read_ref, u_write_ref):
        xb = x_ref[...].astype(BF16)
        phases = make_phases(u_read_ref)
        total = sum(w for _, w in phases)
        done, acc = 0, 0
        for phase, weight in phases:
            acc += weight
            upto = -(-acc * len(chunks) // total)
            if not proj_first:
                phase()
            for lo, hi in chunks[done:upto]:
                u_write_ref[:, lo:hi] = _dot(xb, win_ref[:, lo:hi])
            done = max(done, upto)
            if proj_first:
                phase()

    parity = lax.rem(i, 2)

    @pl.when(parity == 0)
    def _():
        step(u_b_ref, u_a_ref)

    @pl.when(parity == 1)
    def _():
        step(u_a_ref, u_b_ref)


def _skewed_mixer_call(body, x2, batch, layer, kbd, vbd, operands, u_width, extra_scratch):
    n, d = x2.shape
    seq = n // batch
    tm = min(TM_MIX, seq)
    nt = seq // tm
    n_tiles = n // tm

    def prev(i):
        return jnp.maximum(i - 1, 0)

    in_specs = [
        pl.BlockSpec((tm, d), lambda i: (jnp.minimum(i, n_tiles - 1), 0)),
        pl.BlockSpec((tm, d), lambda i: (prev(i), 0)),
        pl.BlockSpec((1, 1) + kbd.shape[2:], lambda i: (layer, prev(i) // nt, 0, 0)),
        pl.BlockSpec((1, 1) + vbd.shape[2:], lambda i: (layer, prev(i) // nt, 0, 0)),
    ] + [_const_spec(a.shape) for a in operands]
    return pl.pallas_call(
        functools.partial(body, nt),
        grid=(n_tiles + 1,),
        in_specs=in_specs,
        out_specs=pl.BlockSpec((tm, d), lambda i: (prev(i), 0)),
        out_shape=jax.ShapeDtypeStruct((n, d), F32),
        scratch_shapes=[pltpu.VMEM((tm, u_width), F32), pltpu.VMEM((tm, u_width), F32)] + extra_scratch,
        compiler_params=pltpu.CompilerParams(
            dimension_semantics=("arbitrary",), vmem_limit_bytes=VMEM_LIMIT_BYTES),
    )(x2, x2, kbd, vbd, *operands)


def _causal_taps(buf_ref, cur_tile, w_ref, tm, tiles):
    width = w_ref.shape[0]
    outs = []
    for c in tiles:
        lanes = slice(c * LANES, (c + 1) * LANES)
        cur_c = cur_tile(c)
        buf_ref[c, pl.ds(SUBLANES, tm), :] = cur_c
        out = cur_c * w_ref[width - 1:width, lanes]
        for j in range(width - 1):
            off = SUBLANES - (width - 1) + j
            out = out + buf_ref[c, pl.ds(off, tm), :] * w_ref[j:j + 1, lanes]
        buf_ref[c, pl.ds(0, SUBLANES), :] = buf_ref[c, pl.ds(tm, SUBLANES), :]
        outs.append(out)
    return jnp.concatenate(outs, axis=1)


def _mix_conv_kernel(alpha, nt, x_ref, xp_ref, kbd_ref, vbd_ref, win_ref, convw_ref, wout_ref, g_ref,
                     b_ref, o_ref, u_ref, u_next_ref, zbuf_ref):
    tm = x_ref.shape[0]
    d_tok = convw_ref.shape[1]

    def reset():
        zbuf_ref[:, pl.ds(0, SUBLANES), :] = jnp.zeros((zbuf_ref.shape[0], SUBLANES, LANES), F32)

    def make_phases(u_ref):
        st = {}

        def z_tile(c):
            lo = d_tok + c * LANES
            return u_ref[:, lo:lo + LANES] * u_ref[:, d_tok + lo:d_tok + lo + LANES]

        def conv():
            conv = _causal_taps(zbuf_ref, z_tile, convw_ref, tm, range(d_tok // LANES))
            st["tok"] = (u_ref[:, :d_tok] * conv).astype(BF16)

        def attention():
            st["xa"] = _cross_attention(u_ref[:, 3 * d_tok:], kbd_ref[0, 0], vbd_ref[0, 0]).astype(BF16)

        def out_proj():
            st["mix"] = _dot(jnp.concatenate([st["tok"], st["xa"]], axis=1), wout_ref[...])

        def norm():
            o_ref[...] = _layer_norm(alpha * xp_ref[...] + st["mix"], g_ref[...], b_ref[...])

        return list(zip([conv, attention, out_proj, norm], CONV_PHASE_WEIGHTS))

    _skewed_body(nt, x_ref, win_ref, u_ref, u_next_ref, reset, make_phases, proj_first=False)


def _mix_conv(x2, batch, win, convw, kbd, vbd, layer, wout, g, b, alpha):
    d_tok = convw.shape[1]
    tm = min(TM_MIX, x2.shape[0] // batch)
    return _skewed_mixer_call(
        functools.partial(_mix_conv_kernel, alpha), x2, batch, layer, kbd, vbd,
        [win, convw, wout, g, b], win.shape[1],
        [pltpu.VMEM((d_tok // LANES, SUBLANES + tm, LANES), F32)])


def _head_window(h, dh):
    pair_start = (h // 2) * 2 * dh
    if h % 2 == 0:
        return pair_start, 0, dh
    return pair_start + 2 * dh - MXU_DIM, MXU_DIM - dh, 0


def _mix_mlstm_kernel(alpha, nt, x_ref, xp_ref, kbd_ref, vbd_ref, win_ref, convw_ref, bg_ref, hng_ref,
                      wout_ref, g_ref, b_ref, o_ref, u_ref, u_next_ref, cbuf_ref, c_ref, m_ref):
    tm = x_ref.shape[0]
    d_qk = convw_ref.shape[1]
    d_tok = d_qk // 2
    dh = d_tok // ML_HEADS
    length = ML_CHUNK
    width = MXU_DIM
    n_chunks = tm // length
    qk_tiles = d_qk // LANES
    heads = range(ML_HEADS)
    pairs = [(c, h) for c in range(n_chunks) for h in heads]

    def reset():
        cbuf_ref[:, pl.ds(0, SUBLANES), :] = jnp.zeros((cbuf_ref.shape[0], SUBLANES, LANES), F32)
        c_ref[...] = jnp.zeros(c_ref.shape, F32)
        m_ref[...] = jnp.zeros(m_ref.shape, F32)

    def make_phases(u_ref):
        st = {}
        window = [_head_window(h, dh) for h in heads]

        def rows(c):
            return slice(c * length, (c + 1) * length)

        def u_tile(c):
            return u_ref[:, c * LANES:(c + 1) * LANES]

        def conv(part):
            def phase():
                lo = part * qk_tiles // 4
                a = _causal_taps(cbuf_ref, u_tile, convw_ref, tm, range(lo, lo + qk_tiles // 4))
                a = a * _sigmoid(a)
                st["qk", part] = a.astype(BF16) if part < 2 else a * (dh ** -0.5)
                if part == 1:
                    st["q"] = jnp.concatenate([st.pop(("qk", 0)), st.pop(("qk", 1))], axis=1)
                if part == 3:
                    st["k"] = jnp.concatenate([st.pop(("qk", 2)), st.pop(("qk", 3))], axis=1)
            return phase

        def gate_math():
            gates = u_ref[:, d_qk + 2 * d_tok:d_qk + 2 * d_tok + GATE_PAD] + bg_ref[...]
            glane = lax.broadcasted_iota(jnp.int32, gates.shape, 1)
            log_sig = jnp.minimum(gates, 0.0) - jnp.log(1.0 + jnp.exp(-jnp.abs(gates)))
            gl = jnp.where(glane < ML_HEADS, gates, log_sig)
            row = lax.broadcasted_iota(jnp.int32, (length, length), 0)
            col = lax.broadcasted_iota(jnp.int32, (length, length), 1)
            st["tri"] = row >= col
            tri_f = st["tri"].astype(F32)
            wlane = lax.broadcasted_iota(jnp.int32, (length, width), 1)
            st["valid"] = [(wlane >= first) & (wlane < first + dh) for _, first, _ in window]
            st["spare"] = [wlane == spare for _, _, spare in window]
            for c in range(n_chunks):
                gc = gl[rows(c)]
                cum = jnp.dot(tri_f, gc, preferred_element_type=F32,
                              precision=lax.Precision.HIGHEST)
                gct, cumt = gc.T, cum.T
                for h in heads:
                    st["icol", c, h] = gc[:, h:h + 1]
                    st["bcol", c, h] = cum[:, ML_HEADS + h:ML_HEADS + h + 1]
                    st["irow", c, h] = gct[h:h + 1, :]
                    st["brow", c, h] = cumt[ML_HEADS + h:ML_HEADS + h + 1, :]
                    st["blast", c, h] = cum[length - 1:length, ML_HEADS + h:ML_HEADS + h + 1]

        def decay_weights():
            for c, h in pairs:
                st["logd", c, h] = jnp.where(
                    st["tri"], st["bcol", c, h] - st["brow", c, h] + st["irow", c, h], NEG_BIG)
            for c, h in pairs:
                st["a", c, h] = jnp.max(st["logd", c, h], axis=-1, keepdims=True)
            for c, h in pairs:
                st["p", c, h] = jnp.exp(st.pop(("logd", c, h)) - st["a", c, h])

        def scores():
            for c, h in pairs:
                start = window[h][0]
                st["qw", c, h] = st["q"][rows(c), start:start + width]
                st["kw", c, h] = jnp.where(st["valid"][h], st["k"][rows(c), start:start + width],
                                           0.0).astype(BF16)
            for c, h in pairs:
                s = _dot_nt(st["qw", c, h], st["kw", c, h])
                st["sc", c, h] = (s * st.pop(("p", c, h))).astype(BF16)

        def intra():
            for c, h in pairs:
                start = window[h][0]
                vw = u_ref[rows(c), d_qk + start:d_qk + start + width]
                st["vaug", c, h] = jnp.where(st["spare"][h], 1.0, vw)
            for c, h in pairs:
                st["intra", c, h] = _dot(st.pop(("sc", c, h)), st["vaug", c, h].astype(BF16))

        def recur(c):
            def combine():
                m = [m_ref[h, 0:1, 0:1] for h in heads]
                c_st = [c_ref[h] for h in heads]
                inter = [_dot(st["qw", c, h], c_st[h].astype(BF16)) for h in heads]
                log_inter = [st["bcol", c, h] + m[h] for h in heads]
                m_t = [jnp.maximum(log_inter[h], st["a", c, h]) for h in heads]
                num = [jnp.exp(st["a", c, h] - m_t[h]) * st.pop(("intra", c, h))
                       + jnp.exp(log_inter[h] - m_t[h]) * inter[h] for h in heads]
                den = [jnp.sum(jnp.where(st["spare"][h], num[h], 0.0), axis=-1, keepdims=True)
                       for h in heads]
                for h in heads:
                    st["m", h], st["c_st", h] = m[h], c_st[h]
                    st["hw", h] = num[h] / jnp.maximum(jnp.abs(den[h]), jnp.exp(-m_t[h]))

            def update():
                m = [st.pop(("m", h)) for h in heads]
                blast = [st["blast", c, h] for h in heads]
                log_w_row = [blast[h] - st["brow", c, h] + st["irow", c, h] for h in heads]
                m_new = [jnp.maximum(blast[h] + m[h], jnp.max(log_w_row[h], axis=-1, keepdims=True))
                         for h in heads]
                w_k = [jnp.exp(blast[h] - st["bcol", c, h] + st["icol", c, h] - m_new[h])
                       for h in heads]
                upd = [_dot_tn(st["kw", c, h], (w_k[h] * st.pop(("vaug", c, h))).astype(BF16))
                       for h in heads]
                for h in heads:
                    c_ref[h] = jnp.exp(blast[h] + m[h] - m_new[h]) * st.pop(("c_st", h)) + upd[h]
                    m_ref[h] = jnp.broadcast_to(m_new[h], m_ref.shape[1:])

            def head_norm():
                hw = [st.pop(("hw", h)) for h in heads]
                mu = [jnp.sum(jnp.where(st["valid"][h], hw[h], 0.0), axis=-1, keepdims=True) / dh
                      for h in heads]
                d = [jnp.where(st["valid"][h], hw[h] - mu[h], 0.0) for h in heads]
                var = [jnp.sum(d[h] * d[h], axis=-1, keepdims=True) / dh for h in heads]
                for h in heads:
                    st["win", c, h] = d[h] * lax.rsqrt(var[h] + LN_EPS)

            return [(combine, 1), (update, 0), (head_norm, 1)]

        def tail_gate():
            hn_rows = []
            for c in range(n_chunks):
                pieces = []
                for p in range(ML_HEADS // 2):
                    even, odd = st["win", c, 2 * p], st["win", c, 2 * p + 1]
                    pieces += [even[:, :LANES], even[:, LANES:] + odd[:, :LANES], odd[:, LANES:]]
                hn_rows.append(jnp.concatenate(pieces, axis=1))
            hn = hn_rows[0] if n_chunks == 1 else jnp.concatenate(hn_rows, axis=0)
            o_pre = u_ref[:, d_qk + d_tok:d_qk + 2 * d_tok]
            st["tok"] = (_sigmoid(o_pre) * (hn * hng_ref[...])).astype(BF16)

        def tail_attention():
            q_mem = u_ref[:, d_qk + 2 * d_tok + GATE_PAD:]
            xa = _cross_attention(q_mem, kbd_ref[0, 0], vbd_ref[0, 0])
            cat = jnp.concatenate([st["tok"], xa.astype(BF16)], axis=1)
            st["mix"] = _dot(cat, wout_ref[...])

        def norm():
            o_ref[...] = _layer_norm(alpha * xp_ref[...] + st["mix"], g_ref[...], b_ref[...])

        recurs = [p for c in range(n_chunks) for p, _ in recur(c)]
        phases = ([conv(part) for part in range(4)] + [gate_math, decay_weights, scores, intra]
                  + recurs + [tail_gate, tail_attention, norm])
        weights = (MLSTM_PHASE_WEIGHTS[:8] + MLSTM_PHASE_WEIGHTS[8:11] * n_chunks
                   + MLSTM_PHASE_WEIGHTS[11:])
        return list(zip(phases, weights))

    _skewed_body(nt, x_ref, win_ref, u_ref, u_next_ref, reset, make_phases, proj_first=True)


def _mix_mlstm(x2, batch, win, convw, bg, hng, kbd, vbd, layer, wout, g, b, alpha):
    d_qk = convw.shape[1]
    tm = min(TM_MIX, x2.shape[0] // batch)
    return _skewed_mixer_call(
        functools.partial(_mix_mlstm_kernel, alpha), x2, batch, layer, kbd, vbd,
        [win, convw, bg, hng, wout, g, b], win.shape[1],
        [pltpu.VMEM((d_qk // LANES, SUBLANES + tm, LANES), F32),
         pltpu.VMEM((ML_HEADS, MXU_DIM, MXU_DIM), F32),
         pltpu.VMEM((ML_HEADS, SUBLANES, LANES), F32)])


def _prep_mlstm(w, b_gates, head_norm_g, d_tok):
    n_gate = 2 * ML_HEADS
    gate_w = jnp.pad(w[:, 4 * d_tok:4 * d_tok + n_gate], ((0, 0), (0, GATE_PAD - n_gate)))
    win = jnp.concatenate([w[:, :4 * d_tok], gate_w, w[:, 4 * d_tok + n_gate:]], axis=1).astype(BF16)
    bg = jnp.pad(b_gates, (0, GATE_PAD - n_gate))[None, :]
    return win, bg, head_norm_g.reshape(1, d_tok)


def kernel(x, mem, ln_g, ln_b, ffn_w_gate, ffn_w_up, ffn_w_down, w_kv_mem, w_out, w_in_conv, conv_w,
           w_in_mlstm, b_gates, qk_conv_w, head_norm_g):
    batch, seq, d = x.shape
    depth = ffn_w_gate.shape[0]
    d_xa = w_kv_mem.shape[-1] // 2
    d_tok = conv_w.shape[-1]
    n_mixers = 2
    alpha = (2.0 * depth) ** 0.25
    assert seq % ML_CHUNK == 0 and d_tok % ML_HEADS == 0
    assert 2 * (d_tok // ML_HEADS) % LANES == 0 and d_tok // ML_HEADS <= MXU_DIM

    wg = ffn_w_gate.astype(BF16)
    wu = ffn_w_up.astype(BF16)
    wd = ffn_w_down.astype(BF16)
    wo = w_out.astype(BF16)
    wkt = jnp.swapaxes(w_kv_mem[:, :, :d_xa], 1, 2).astype(BF16)
    wv = w_kv_mem[:, :, d_xa:].astype(BF16)
    kbd, vbd = _mem_kv(mem, wkt, wv)

    x2 = x.reshape(batch * seq, d)
    for l in range(depth):
        g = ln_g[l][:, None, :]
        b = ln_b[l][:, None, :]
        x2 = _ffn_ln(x2, wg, wu, wd, l, 0, g[0], b[0], alpha)
        j = l // n_mixers
        if l % n_mixers == 0:
            x2 = _mix_conv(x2, batch, w_in_conv[j].astype(BF16), conv_w[j], kbd, vbd, l, wo[l],
                           g[1], b[1], alpha)
        else:
            win, bg, hng = _prep_mlstm(w_in_mlstm[j], b_gates[j], head_norm_g[j], d_tok)
            x2 = _mix_mlstm(x2, batch, win, qk_conv_w[j], bg, hng, kbd, vbd, l, wo[l],
                            g[1], b[1], alpha)
        x2 = _ffn_ln(x2, wg, wu, wd, l, 1, g[2], b[2], alpha)
    return x2.reshape(batch, seq, d)
```

```python
import functools

import jax
import jax.numpy as jnp
from jax import lax
from jax.experimental import pallas as pl
from jax.experimental.pallas import tpu as pltpu

F32 = jnp.float32
BF16 = jnp.bfloat16

XA_HEADS = 4
CONV_WIDTH = 3
QK_CONV_WIDTH = 4
ML_HEADS = 4
LN_EPS = 1e-5

LANES = 128
SUBLANES = 8
MXU_DIM = 256
VMEM_LIMIT_BYTES = 56 * 1024 * 1024

TM_FFN = 1024
FFN_SUB = 2
FFN_FF_CHUNKS = (1536, 1280)
TM_MIX = 512
PROJ_CHUNK = 256
CONV_PHASE_WEIGHTS = (1, 3, 2, 4)
MLSTM_PHASE_WEIGHTS = (1, 1, 1, 1, 1, 1, 1, 0, 1, 0, 1, 1, 1, 1)
ML_CHUNK = MXU_DIM
GATE_PAD = LANES
NEG_BIG = -1e30


def _layer_norm(r, g, b):
    mu = jnp.mean(r, axis=-1, keepdims=True)
    d = r - mu
    var = jnp.mean(d * d, axis=-1, keepdims=True)
    return d * lax.rsqrt(var + LN_EPS) * g + b


def _sigmoid(x):
    return 1.0 / (1.0 + jnp.exp(-x))


def _dot(a, b):
    return jnp.dot(a, b, preferred_element_type=F32)


def _dot_nt(a, b):
    return lax.dot_general(a, b, (((1,), (1,)), ((), ())), preferred_element_type=F32)


def _dot_tn(a, b):
    return lax.dot_general(a, b, (((0,), (0,)), ((), ())), preferred_element_type=F32)


def _const_spec(shape):
    nd = len(shape)
    return pl.BlockSpec(shape, lambda *_: (0,) * nd)


def _ffn_ln_kernel(alpha, n_sub, x_ref, wg_ref, wu_ref, wd_ref, g_ref, b_ref, o_ref):
    ts = x_ref.shape[0] // n_sub
    xb = x_ref[...].astype(BF16)
    hs, lo = [], 0
    for width in FFN_FF_CHUNKS:
        cols = slice(lo, lo + width)
        gate = _dot(xb, wg_ref[:, cols])
        up = _dot(xb, wu_ref[:, cols])
        hs.append((gate * _sigmoid(gate) * up).astype(BF16))
        lo += width
    h = jnp.concatenate(hs, axis=1)
    pending = None
    for s in range(n_sub):
        y = _dot(h[s * ts:(s + 1) * ts], wd_ref[...])
        if pending is not None:
            prows, r = pending
            o_ref[prows, :] = _layer_norm(r, g_ref[...], b_ref[...])
        rows = pl.ds(s * ts, ts)
        pending = (rows, alpha * x_ref[rows, :] + 0.5 * y)
    prows, r = pending
    o_ref[prows, :] = _layer_norm(r, g_ref[...], b_ref[...])


def _ffn_ln(x2, wg, wu, wd, layer, pos, g, b, alpha):
    n, d = x2.shape
    d_ff = wg.shape[-1]
    assert sum(FFN_FF_CHUNKS) == d_ff and all(w % MXU_DIM == 0 for w in FFN_FF_CHUNKS)
    tm = min(TM_FFN, n)
    return pl.pallas_call(
        functools.partial(_ffn_ln_kernel, alpha, FFN_SUB),
        grid=(n // tm,),
        in_specs=[
            pl.BlockSpec((tm, d), lambda i: (i, 0)),
            pl.BlockSpec((None, None, d, d_ff), lambda i: (layer, pos, 0, 0)),
            pl.BlockSpec((None, None, d, d_ff), lambda i: (layer, pos, 0, 0)),
            pl.BlockSpec((None, None, d_ff, d), lambda i: (layer, pos, 0, 0)),
            _const_spec((1, d)),
            _const_spec((1, d)),
        ],
        out_specs=pl.BlockSpec((tm, d), lambda i: (i, 0)),
        out_shape=jax.ShapeDtypeStruct((n, d), F32),
        compiler_params=pltpu.CompilerParams(
            dimension_semantics=("arbitrary",), vmem_limit_bytes=VMEM_LIMIT_BYTES),
    )(x2, wg, wu, wd, g, b)


def _mem_kv_kernel(scale, mem_ref, wkt_ref, wv_ref, kbd_ref, vbd_ref):
    memb = mem_ref[0].astype(BF16)
    n_mem = memb.shape[0]
    d_xa = wv_ref.shape[-1]
    hd = d_xa // XA_HEADS
    kt = _dot_nt(wkt_ref[0], memb) * scale
    v = _dot(memb, wv_ref[0])
    kt4 = jnp.concatenate([kt] * XA_HEADS, axis=1)
    r = lax.broadcasted_iota(jnp.int32, kt4.shape, 0) // hd
    c = lax.broadcasted_iota(jnp.int32, kt4.shape, 1) // n_mem
    kbd_ref[0, 0] = jnp.where(r == c, kt4, 0.0).astype(BF16)
    v4 = jnp.concatenate([v] * XA_HEADS, axis=0)
    r = lax.broadcasted_iota(jnp.int32, v4.shape, 0) // n_mem
    c = lax.broadcasted_iota(jnp.int32, v4.shape, 1) // hd
    vbd_ref[0, 0] = jnp.where(r == c, v4, 0.0).astype(BF16)


def _mem_kv(mem, wkt, wv):
    b, n_mem, d = mem.shape
    depth, d_xa, _ = wkt.shape
    hd = d_xa // XA_HEADS
    return pl.pallas_call(
        functools.partial(_mem_kv_kernel, hd ** -0.5),
        grid=(depth, b),
        in_specs=[
            pl.BlockSpec((1, n_mem, d), lambda l, i: (i, 0, 0)),
            pl.BlockSpec((1, d_xa, d), lambda l, i: (l, 0, 0)),
            pl.BlockSpec((1, d, d_xa), lambda l, i: (l, 0, 0)),
        ],
        out_specs=[
            pl.BlockSpec((1, 1, d_xa, XA_HEADS * n_mem), lambda l, i: (l, i, 0, 0)),
            pl.BlockSpec((1, 1, XA_HEADS * n_mem, d_xa), lambda l, i: (l, i, 0, 0)),
        ],
        out_shape=[
            jax.ShapeDtypeStruct((depth, b, d_xa, XA_HEADS * n_mem), BF16),
            jax.ShapeDtypeStruct((depth, b, XA_HEADS * n_mem, d_xa), BF16),
        ],
        compiler_params=pltpu.CompilerParams(
            dimension_semantics=("arbitrary", "arbitrary"), vmem_limit_bytes=VMEM_LIMIT_BYTES),
    )(mem, wkt, wv)


def _cross_attention(q, kbd, vbd):
    tm, d_xa = q.shape
    n_mem = kbd.shape[1] // XA_HEADS
    hd = d_xa // XA_HEADS
    s = _dot(q.astype(BF16), kbd)
    heads = range(XA_HEADS)
    sh = [s[:, h * n_mem:(h + 1) * n_mem] for h in heads]
    mx = [jnp.max(sh[h], axis=-1, keepdims=True) for h in heads]
    p = [jnp.exp(sh[h] - mx[h]) for h in heads]
    ls = [jnp.sum(p[h], axis=-1, keepdims=True) for h in heads]
    o = _dot(jnp.concatenate([p[h].astype(BF16) for h in heads], axis=1), vbd)
    lane_head = lax.broadcasted_iota(jnp.int32, (tm, d_xa), 1) // hd
    l = jnp.broadcast_to(ls[XA_HEADS - 1], (tm, d_xa))
    for h in range(XA_HEADS - 2, -1, -1):
        l = jnp.where(lane_head == h, ls[h], l)
    return o / l


def _skewed_body(nt, x_ref, win_ref, u_a_ref, u_b_ref, reset_fn, make_phases, proj_first):
    i = pl.program_id(0)

    @pl.when(i == 0)
    def _():
        u_b_ref[...] = jnp.zeros(u_b_ref.shape, F32)

    @pl.when(lax.rem(jnp.maximum(i - 1, 0), nt) == 0)
    def _():
        reset_fn()

    width = win_ref.shape[1]
    chunks = [(lo, min(lo + PROJ_CHUNK, width)) for lo in range(0, width, PROJ_CHUNK)]

    def step(u_read_ref, u_write_ref):
        xb = x_ref[...].astype(BF16)
        phases = make_phases(u_read_ref)
        total = sum(w for _, w in phases)
        done, acc = 0, 0
        for phase, weight in phases:
            acc += weight
            upto = -(-acc * len(chunks) // total)
            if not proj_first:
                phase()
            for lo, hi in chunks[done:upto]:
                u_write_ref[:, lo:hi] = _dot(xb, win_ref[:, lo:hi])
            done = max(done, upto)
            if proj_first:
                phase()

    parity = lax.rem(i, 2)

    @pl.when(parity == 0)
    def _():
        step(u_b_ref, u_a_ref)

    @pl.when(parity == 1)
    def _():
        step(u_a_ref, u_b_ref)


def _skewed_mixer_call(body, x2, batch, layer, kbd, vbd, operands, u_width, extra_scratch):
    n, d = x2.shape
    seq = n // batch
    tm = min(TM_MIX, seq)
    nt = seq // tm
    n_tiles = n // tm

    def prev(i):
        return jnp.maximum(i - 1, 0)

    in_specs = [
        pl.BlockSpec((tm, d), lambda i: (jnp.minimum(i, n_tiles - 1), 0)),
        pl.BlockSpec((tm, d), lambda i: (prev(i), 0)),
        pl.BlockSpec((1, 1) + kbd.shape[2:], lambda i: (layer, prev(i) // nt, 0, 0)),
        pl.BlockSpec((1, 1) + vbd.shape[2:], lambda i: (layer, prev(i) // nt, 0, 0)),
    ] + [_const_spec(a.shape) for a in operands]
    return pl.pallas_call(
        functools.partial(body, nt),
        grid=(n_tiles + 1,),
        in_specs=in_specs,
        out_specs=pl.BlockSpec((tm, d), lambda i: (prev(i), 0)),
        out_shape=jax.ShapeDtypeStruct((n, d), F32),
        scratch_shapes=[pltpu.VMEM((tm, u_width), F32), pltpu.VMEM((tm, u_width), F32)] + extra_scratch,
        compiler_params=pltpu.CompilerParams(
            dimension_semantics=("arbitrary",), vmem_limit_bytes=VMEM_LIMIT_BYTES),
    )(x2, x2, kbd, vbd, *operands)


def _causal_taps(buf_ref, cur_tile, w_ref, tm, tiles):
    width = w_ref.shape[0]
    outs = []
    for c in tiles:
        lanes = slice(c * LANES, (c + 1) * LANES)
        cur_c = cur_tile(c)
        buf_ref[c, pl.ds(SUBLANES, tm), :] = cur_c
        out = cur_c * w_ref[width - 1:width, lanes]
        for j in range(width - 1):
            off = SUBLANES - (width - 1) + j
            out = out + buf_ref[c, pl.ds(off, tm), :] * w_ref[j:j + 1, lanes]
        buf_ref[c, pl.ds(0, SUBLANES), :] = buf_ref[c, pl.ds(tm, SUBLANES), :]
        outs.append(out)
    return jnp.concatenate(outs, axis=1)


def _mix_conv_kernel(alpha, nt, x_ref, xp_ref, kbd_ref, vbd_ref, win_ref, convw_ref, wout_ref, g_ref,
                     b_ref, o_ref, u_ref, u_next_ref, zbuf_ref):
    tm = x_ref.shape[0]
    d_tok = convw_ref.shape[1]

    def reset():
        zbuf_ref[:, pl.ds(0, SUBLANES), :] = jnp.zeros((zbuf_ref.shape[0], SUBLANES, LANES), F32)

    def make_phases(u_ref):
        st = {}

        def z_tile(c):
            lo = d_tok + c * LANES
            return u_ref[:, lo:lo + LANES] * u_ref[:, d_tok + lo:d_tok + lo + LANES]

        def conv():
            conv = _causal_taps(zbuf_ref, z_tile, convw_ref, tm, range(d_tok // LANES))
            st["tok"] = (u_ref[:, :d_tok] * conv).astype(BF16)

        def attention():
            st["xa"] = _cross_attention(u_ref[:, 3 * d_tok:], kbd_ref[0, 0], vbd_ref[0, 0]).astype(BF16)

        def out_proj():
            st["mix"] = _dot(jnp.concatenate([st["tok"], st["xa"]], axis=1), wout_ref[...])

        def norm():
            o_ref[...] = _layer_norm(alpha * xp_ref[...] + st["mix"], g_ref[...], b_ref[...])

        return list(zip([conv, attention, out_proj, norm], CONV_PHASE_WEIGHTS))

    _skewed_body(nt, x_ref, win_ref, u_ref, u_next_ref, reset, make_phases, proj_first=False)


def _mix_conv(x2, batch, win, convw, kbd, vbd, layer, wout, g, b, alpha):
    d_tok = convw.shape[1]
    tm = min(TM_MIX, x2.shape[0] // batch)
    return _skewed_mixer_call(
        functools.partial(_mix_conv_kernel, alpha), x2, batch, layer, kbd, vbd,
        [win, convw, wout, g, b], win.shape[1],
        [pltpu.VMEM((d_tok // LANES, SUBLANES + tm, LANES), F32)])


def _head_window(h, dh):
    pair_start = (h // 2) * 2 * dh
    if h % 2 == 0:
        return pair_start, 0, dh
    return pair_start + 2 * dh - MXU_DIM, MXU_DIM - dh, 0


def _mix_mlstm_kernel(alpha, nt, x_ref, xp_ref, kbd_ref, vbd_ref, win_ref, convw_ref, bg_ref, hng_ref,
                      wout_ref, g_ref, b_ref, o_ref, u_ref, u_next_ref, cbuf_ref, c_ref, m_ref):
    tm = x_ref.shape[0]
    d_qk = convw_ref.shape[1]
    d_tok = d_qk // 2
    dh = d_tok // ML_HEADS
    length = ML_CHUNK
    width = MXU_DIM
    n_chunks = tm // length
    qk_tiles = d_qk // LANES
    heads = range(ML_HEADS)
    pairs = [(c, h) for c in range(n_chunks) for h in heads]

    def reset():
        cbuf_ref[:, pl.ds(0, SUBLANES), :] = jnp.zeros((cbuf_ref.shape[0], SUBLANES, LANES), F32)
        c_ref[...] = jnp.zeros(c_ref.shape, F32)
        m_ref[...] = jnp.zeros(m_ref.shape, F32)

    def make_phases(u_ref):
        st = {}
        window = [_head_window(h, dh) for h in heads]

        def rows(c):
            return slice(c * length, (c + 1) * length)

        def u_tile(c):
            return u_ref[:, c * LANES:(c + 1) * LANES]

        def conv(part):
            def phase():
                lo = part * qk_tiles // 4
                a = _causal_taps(cbuf_ref, u_tile, convw_ref, tm, range(lo, lo + qk_tiles // 4))
                a = a * _sigmoid(a)
                st["qk", part] = a.astype(BF16) if part < 2 else a * (dh ** -0.5)
                if part == 1:
                    st["q"] = jnp.concatenate([st.pop(("qk", 0)), st.pop(("qk", 1))], axis=1)
                if part == 3:
                    st["k"] = jnp.concatenate([st.pop(("qk", 2)), st.pop(("qk", 3))], axis=1)
            return phase

        def gate_math():
            gates = u_ref[:, d_qk + 2 * d_tok:d_qk + 2 * d_tok + GATE_PAD] + bg_ref[...]
            glane = lax.broadcasted_iota(jnp.int32, gates.shape, 1)
            log_sig = jnp.minimum(gates, 0.0) - jnp.log(1.0 + jnp.exp(-jnp.abs(gates)))
            gl = jnp.where(glane < ML_HEADS, gates, log_sig)
            row = lax.broadcasted_iota(jnp.int32, (length, length), 0)
            col = lax.broadcasted_iota(jnp.int32, (length, length), 1)
            st["tri"] = row >= col
            tri_f = st["tri"].astype(F32)
            wlane = lax.broadcasted_iota(jnp.int32, (length, width), 1)
            st["valid"] = [(wlane >= first) & (wlane < first + dh) for _, first, _ in window]
            st["spare"] = [wlane == spare for _, _, spare in window]
            for c in range(n_chunks):
                gc = gl[rows(c)]
                cum = jnp.dot(tri_f, gc, preferred_element_type=F32,
                              precision=lax.Precision.HIGHEST)
                gct, cumt = gc.T, cum.T
                for h in heads:
                    st["icol", c, h] = gc[:, h:h + 1]
                    st["bcol", c, h] = cum[:, ML_HEADS + h:ML_HEADS + h + 1]
                    st["irow", c, h] = gct[h:h + 1, :]
                    st["brow", c, h] = cumt[ML_HEADS + h:ML_HEADS + h + 1, :]
                    st["blast", c, h] = cum[length - 1:length, ML_HEADS + h:ML_HEADS + h + 1]

        def decay_weights():
            for c, h in pairs:
                st["logd", c, h] = jnp.where(
                    st["tri"], st["bcol", c, h] - st["brow", c, h] + st["irow", c, h], NEG_BIG)
            for c, h in pairs:
                st["a", c, h] = jnp.max(st["logd", c, h], axis=-1, keepdims=True)
            for c, h in pairs:
                st["p", c, h] = jnp.exp(st.pop(("logd", c, h)) - st["a", c, h])

        def scores():
            for c, h in pairs:
                start = window[h][0]
                st["qw", c, h] = st["q"][rows(c), start:start + width]
                st["kw", c, h] = jnp.where(st["valid"][h], st["k"][rows(c), start:start + width],
                                           0.0).astype(BF16)
            for c, h in pairs:
                s = _dot_nt(st["qw", c, h], st["kw", c, h])
                st["sc", c, h] = (s * st.pop(("p", c, h))).astype(BF16)

        def intra():
            for c, h in pairs:
                start = window[h][0]
                vw = u_ref[rows(c), d_qk + start:d_qk + start + width]
                st["vaug", c, h] = jnp.where(st["spare"][h], 1.0, vw)
            for c, h in pairs:
                st["intra", c, h] = _dot(st.pop(("sc", c, h)), st["vaug", c, h].astype(BF16))

        def recur(c):
            def combine():
                m = [m_ref[h, 0:1, 0:1] for h in heads]
                c_st = [c_ref[h] for h in heads]
                inter = [_dot(st["qw", c, h], c_st[h].astype(BF16)) for h in heads]
                log_inter = [st["bcol", c, h] + m[h] for h in heads]
                m_t = [jnp.maximum(log_inter[h], st["a", c, h]) for h in heads]
                num = [jnp.exp(st["a", c, h] - m_t[h]) * st.pop(("intra", c, h))
                       + jnp.exp(log_inter[h] - m_t[h]) * inter[h] for h in heads]
                den = [jnp.sum(jnp.where(st["spare"][h], num[h], 0.0), axis=-1, keepdims=True)
                       for h in heads]
                for h in heads:
                    st["m", h], st["c_st", h] = m[h], c_st[h]
                    st["hw", h] = num[h] / jnp.maximum(jnp.abs(den[h]), jnp.exp(-m_t[h]))

            def update():
                m = [st.pop(("m", h)) for h in heads]
                blast = [st["blast", c, h] for h in heads]
                log_w_row = [blast[h] - st["brow", c, h] + st["irow", c, h] for h in heads]
                m_new = [jnp.maximum(blast[h] + m[h], jnp.max(log_w_row[h], axis=-1, keepdims=True))
                         for h in heads]
                w_k = [jnp.exp(blast[h] - st["bcol", c, h] + st["icol", c, h] - m_new[h])
                       for h in heads]
                upd = [_dot_tn(st["kw", c, h], (w_k[h] * st.pop(("vaug", c, h))).astype(BF16))
                       for h in heads]
                for h in heads:
                    c_ref[h] = jnp.exp(blast[h] + m[h] - m_new[h]) * st.pop(("c_st", h)) + upd[h]
                    m_ref[h] = jnp.broadcast_to(m_new[h], m_ref.shape[1:])

            def head_norm():
                hw = [st.pop(("hw", h)) for h in heads]
                mu = [jnp.sum(jnp.where(st["valid"][h], hw[h], 0.0), axis=-1, keepdims=True) / dh
                      for h in heads]
                d = [jnp.where(st["valid"][h], hw[h] - mu[h], 0.0) for h in heads]
                var = [jnp.sum(d[h] * d[h], axis=-1, keepdims=True) / dh for h in heads]
                for h in heads:
                    st["win", c, h] = d[h] * lax.rsqrt(var[h] + LN_EPS)

            return [(combine, 1), (update, 0), (head_norm, 1)]

        def tail_gate():
            hn_rows = []
            for c in range(n_chunks):
                pieces = []
                for p in range(ML_HEADS // 2):
                    even, odd = st["win", c, 2 * p], st["win", c, 2 * p + 1]
                    pieces += [even[:, :LANES], even[:, LANES:] + odd[:, :LANES], odd[:, LANES:]]
                hn_rows.append(jnp.concatenate(pieces, axis=1))
            hn = hn_rows[0] if n_chunks == 1 else jnp.concatenate(hn_rows, axis=0)
            o_pre = u_ref[:, d_qk + d_tok:d_qk + 2 * d_tok]
            st["tok"] = (_sigmoid(o_pre) * (hn * hng_ref[...])).astype(BF16)

        def tail_attention():
            q_mem = u_ref[:, d_qk + 2 * d_tok + GATE_PAD:]
            xa = _cross_attention(q_mem, kbd_ref[0, 0], vbd_ref[0, 0])
            cat = jnp.concatenate([st["tok"], xa.astype(BF16)], axis=1)
            st["mix"] = _dot(cat, wout_ref[...])

        def norm():
            o_ref[...] = _layer_norm(alpha * xp_ref[...] + st["mix"], g_ref[...], b_ref[...])

        recurs = [p for c in range(n_chunks) for p, _ in recur(c)]
        phases = ([conv(part) for part in range(4)] + [gate_math, decay_weights, scores, intra]
                  + recurs + [tail_gate, tail_attention, norm])
        weights = (MLSTM_PHASE_WEIGHTS[:8] + MLSTM_PHASE_WEIGHTS[8:11] * n_chunks
                   + MLSTM_PHASE_WEIGHTS[11:])
        return list(zip(phases, weights))

    _skewed_body(nt, x_ref, win_ref, u_ref, u_next_ref, reset, make_phases, proj_first=True)


def _mix_mlstm(x2, batch, win, convw, bg, hng, kbd, vbd, layer, wout, g, b, alpha):
    d_qk = convw.shape[1]
    tm = min(TM_MIX, x2.shape[0] // batch)
    return _skewed_mixer_call(
        functools.partial(_mix_mlstm_kernel, alpha), x2, batch, layer, kbd, vbd,
        [win, convw, bg, hng, wout, g, b], win.shape[1],
        [pltpu.VMEM((d_qk // LANES, SUBLANES + tm, LANES), F32),
         pltpu.VMEM((ML_HEADS, MXU_DIM, MXU_DIM), F32),
         pltpu.VMEM((ML_HEADS, SUBLANES, LANES), F32)])


def _prep_mlstm(w, b_gates, head_norm_g, d_tok):
    n_gate = 2 * ML_HEADS
    gate_w = jnp.pad(w[:, 4 * d_tok:4 * d_tok + n_gate], ((0, 0), (0, GATE_PAD - n_gate)))
    win = jnp.concatenate([w[:, :4 * d_tok], gate_w, w[:, 4 * d_tok + n_gate:]], axis=1).astype(BF16)
    bg = jnp.pad(b_gates, (0, GATE_PAD - n_gate))[None, :]
    return win, bg, head_norm_g.reshape(1, d_tok)


def kernel(x, mem, ln_g, ln_b, ffn_w_gate, ffn_w_up, ffn_w_down, w_kv_mem, w_out, w_in_conv, conv_w,
           w_in_mlstm, b_gates, qk_conv_w, head_norm_g):
    batch, seq, d = x.shape
    depth = ffn_w_gate.shape[0]
    d_xa = w_kv_mem.shape[-1] // 2
    d_tok = conv_w.shape[-1]
    n_mixers = 2
    alpha = (2.0 * depth) ** 0.25
    assert seq % ML_CHUNK == 0 and d_tok % ML_HEADS == 0
    assert 2 * (d_tok // ML_HEADS) % LANES == 0 and d_tok // ML_HEADS <= MXU_DIM

    wg = ffn_w_gate.astype(BF16)
    wu = ffn_w_up.astype(BF16)
    wd = ffn_w_down.astype(BF16)
    wo = w_out.astype(BF16)
    wkt = jnp.swapaxes(w_kv_mem[:, :, :d_xa], 1, 2).astype(BF16)
    wv = w_kv_mem[:, :, d_xa:].astype(BF16)
    kbd, vbd = _mem_kv(mem, wkt, wv)

    x2 = x.reshape(batch * seq, d)
    for l in range(depth):
        g = ln_g[l][:, None, :]
        b = ln_b[l][:, None, :]
        x2 = _ffn_ln(x2, wg, wu, wd, l, 0, g[0], b[0], alpha)
        j = l // n_mixers
        if l % n_mixers == 0:
            x2 = _mix_conv(x2, batch, w_in_conv[j].astype(BF16), conv_w[j], kbd, vbd, l, wo[l],
                           g[1], b[1], alpha)
        else:
            win, bg, hng = _prep_mlstm(w_in_mlstm[j], b_gates[j], head_norm_g[j], d_tok)
            x2 = _mix_mlstm(x2, batch, win, qk_conv_w[j], bg, hng, kbd, vbd, l, wo[l],
                            g[1], b[1], alpha)
        x2 = _ffn_ln(x2, wg, wu, wd, l, 1, g[2], b[2], alpha)
    return x2.reshape(batch, seq, d)
```

```python
import functools

import jax
import jax.numpy as jnp
from jax import lax
from jax.experimental import pallas as pl
from jax.experimental.pallas import tpu as pltpu

F32 = jnp.float32
BF16 = jnp.bfloat16

XA_HEADS = 4
CONV_WIDTH = 3
QK_CONV_WIDTH = 4
ML_HEADS = 4
LN_EPS = 1e-5

LANES = 128
SUBLANES = 8
MXU_DIM = 256
VMEM_LIMIT_BYTES = 56 * 1024 * 1024

TM_FFN = 1024
FFN_SUB = 2
FFN_FF_CHUNKS = (1536, 1280)
TM_MIX = 512
PROJ_CHUNK = 256
CONV_PHASE_WEIGHTS = (1, 3, 2, 4)
MLSTM_PHASE_WEIGHTS = (1, 1, 1, 1, 1, 1, 1, 0, 1, 0, 1, 1, 1, 1)
ML_CHUNK = MXU_DIM
GATE_PAD = LANES
NEG_BIG = -1e30


def _layer_norm(r, g, b):
    mu = jnp.mean(r, axis=-1, keepdims=True)
    d = r - mu
    var = jnp.mean(d * d, axis=-1, keepdims=True)
    return d * lax.rsqrt(var + LN_EPS) * g + b


def _sigmoid(x):
    return 1.0 / (1.0 + jnp.exp(-x))


def _dot(a, b):
    return jnp.dot(a, b, preferred_element_type=F32)


def _dot_nt(a, b):
    return lax.dot_general(a, b, (((1,), (1,)), ((), ())), preferred_element_type=F32)


def _dot_tn(a, b):
    return lax.dot_general(a, b, (((0,), (0,)), ((), ())), preferred_element_type=F32)


def _const_spec(shape):
    nd = len(shape)
    return pl.BlockSpec(shape, lambda *_: (0,) * nd)


def _ffn_ln_kernel(alpha, n_sub, x_ref, wg_ref, wu_ref, wd_ref, g_ref, b_ref, *rest):
    o_ref = rest[len(rest) // 2]
    ts = x_ref.shape[0] // n_sub
    xb = x_ref[...].astype(BF16)
    hs, lo = [], 0
    for width in FFN_FF_CHUNKS:
        cols = slice(lo, lo + width)
        gate = _dot(xb, wg_ref[:, cols])
        up = _dot(xb, wu_ref[:, cols])
        hs.append((gate * _sigmoid(gate) * up).astype(BF16))
        lo += width
    h = jnp.concatenate(hs, axis=1)
    for src_ref, dst_ref in zip(rest[:len(rest) // 2], rest[len(rest) // 2 + 1:]):
        dst_ref[...] = src_ref[...].astype(BF16)
    pending = None
    for s in range(n_sub):
        y = _dot(h[s * ts:(s + 1) * ts], wd_ref[...])
        if pending is not None:
            prows, r = pending
            o_ref[prows, :] = _layer_norm(r, g_ref[...], b_ref[...])
        rows = pl.ds(s * ts, ts)
        pending = (rows, alpha * x_ref[rows, :] + 0.5 * y)
    prows, r = pending
    o_ref[prows, :] = _layer_norm(r, g_ref[...], b_ref[...])


def _row_pieces(rows, n_steps):
    return max(p for p in range(1, n_steps + 1)
               if n_steps % p == 0 and rows % p == 0 and (rows // p) % (2 * SUBLANES) == 0)


def _ffn_ln(x2, weights, g, b, alpha, next_weights=None):
    n, d = x2.shape
    d_ff = weights[0].shape[-1]
    assert sum(FFN_FF_CHUNKS) == d_ff and all(w % MXU_DIM == 0 for w in FFN_FF_CHUNKS)
    tm = min(TM_FFN, n)
    n_steps = n // tm
    in_specs = [pl.BlockSpec((tm, d), lambda i: (i, 0))]
    in_specs += [_const_spec(w.shape) for w in weights] + [_const_spec((1, d)), _const_spec((1, d))]
    out_specs = [pl.BlockSpec((tm, d), lambda i: (i, 0))]
    out_shape = [jax.ShapeDtypeStruct((n, d), F32)]
    operands = [x2, *weights, g, b]
    if next_weights is not None:
        *params, layer, pos = next_weights
        for w in params:
            rows, cols = w.shape[2:]
            pieces = _row_pieces(rows, n_steps)

            def piece(i, pieces=pieces):
                return i * pieces // n_steps

            in_specs.append(pl.BlockSpec((None, None, rows // pieces, cols),
                                         lambda i, piece=piece: (layer, pos, piece(i), 0)))
            out_specs.append(pl.BlockSpec((rows // pieces, cols), lambda i, piece=piece: (piece(i), 0)))
            out_shape.append(jax.ShapeDtypeStruct((rows, cols), BF16))
            operands.append(w)
    out = pl.pallas_call(
        functools.partial(_ffn_ln_kernel, alpha, FFN_SUB),
        grid=(n_steps,),
        in_specs=in_specs,
        out_specs=out_specs,
        out_shape=out_shape,
        compiler_params=pltpu.CompilerParams(
            dimension_semantics=("arbitrary",), vmem_limit_bytes=VMEM_LIMIT_BYTES),
    )(*operands)
    return out[0], tuple(out[1:])


def _mem_kv_kernel(scale, mem_ref, wkt_ref, wv_ref, kbd_ref, vbd_ref):
    memb = mem_ref[0].astype(BF16)
    n_mem = memb.shape[0]
    d_xa = wv_ref.shape[-1]
    hd = d_xa // XA_HEADS
    kt = _dot_nt(wkt_ref[0], memb) * scale
    v = _dot(memb, wv_ref[0])
    kt4 = jnp.concatenate([kt] * XA_HEADS, axis=1)
    r = lax.broadcasted_iota(jnp.int32, kt4.shape, 0) // hd
    c = lax.broadcasted_iota(jnp.int32, kt4.shape, 1) // n_mem
    kbd_ref[0, 0] = jnp.where(r == c, kt4, 0.0).astype(BF16)
    v4 = jnp.concatenate([v] * XA_HEADS, axis=0)
    r = lax.broadcasted_iota(jnp.int32, v4.shape, 0) // n_mem
    c = lax.broadcasted_iota(jnp.int32, v4.shape, 1) // hd
    vbd_ref[0, 0] = jnp.where(r == c, v4, 0.0).astype(BF16)


def _mem_kv(mem, wkt, wv):
    b, n_mem, d = mem.shape
    depth, d_xa, _ = wkt.shape
    hd = d_xa // XA_HEADS
    return pl.pallas_call(
        functools.partial(_mem_kv_kernel, hd ** -0.5),
        grid=(depth, b),
        in_specs=[
            pl.BlockSpec((1, n_mem, d), lambda l, i: (i, 0, 0)),
            pl.BlockSpec((1, d_xa, d), lambda l, i: (l, 0, 0)),
            pl.BlockSpec((1, d, d_xa), lambda l, i: (l, 0, 0)),
        ],
        out_specs=[
            pl.BlockSpec((1, 1, d_xa, XA_HEADS * n_mem), lambda l, i: (l, i, 0, 0)),
            pl.BlockSpec((1, 1, XA_HEADS * n_mem, d_xa), lambda l, i: (l, i, 0, 0)),
        ],
        out_shape=[
            jax.ShapeDtypeStruct((depth, b, d_xa, XA_HEADS * n_mem), BF16),
            jax.ShapeDtypeStruct((depth, b, XA_HEADS * n_mem, d_xa), BF16),
        ],
        compiler_params=pltpu.CompilerParams(
            dimension_semantics=("arbitrary", "arbitrary"), vmem_limit_bytes=VMEM_LIMIT_BYTES),
    )(mem, wkt, wv)


def _cross_attention(q, kbd, vbd):
    tm, d_xa = q.shape
    n_mem = kbd.shape[1] // XA_HEADS
    hd = d_xa // XA_HEADS
    s = _dot(q.astype(BF16), kbd)
    heads = range(XA_HEADS)
    sh = [s[:, h * n_mem:(h + 1) * n_mem] for h in heads]
    mx = [jnp.max(sh[h], axis=-1, keepdims=True) for h in heads]
    p = [jnp.exp(sh[h] - mx[h]) for h in heads]
    ls = [jnp.sum(p[h], axis=-1, keepdims=True) for h in heads]
    o = _dot(jnp.concatenate([p[h].astype(BF16) for h in heads], axis=1), vbd)
    lane_head = lax.broadcasted_iota(jnp.int32, (tm, d_xa), 1) // hd
    l = jnp.broadcast_to(ls[XA_HEADS - 1], (tm, d_xa))
    for h in range(XA_HEADS - 2, -1, -1):
        l = jnp.where(lane_head == h, ls[h], l)
    return o / l


def _skewed_body(nt, x_ref, win_ref, u_a_ref, u_b_ref, reset_fn, make_phases, proj_first):
    i = pl.program_id(0)

    @pl.when(i == 0)
    def _():
        u_b_ref[...] = jnp.zeros(u_b_ref.shape, F32)

    @pl.when(lax.rem(jnp.maximum(i - 1, 0), nt) == 0)
    def _():
        reset_fn()

    width = win_ref.shape[1]
    chunks = [(lo, min(lo + PROJ_CHUNK, width)) for lo in range(0, width, PROJ_CHUNK)]

    def step(u_read_ref, u_write_ref):
        xb = x_ref[...].astype(BF16)
        phases = make_phases(u_read_ref)
        total = sum(w for _, w in phases)
        done, acc = 0, 0
        for phase, weight in phases:
            acc += weight
            upto = -(-acc * len(chunks) // total)
            if not proj_first:
                phase()
            for lo, hi in chunks[done:upto]:
                u_write_ref[:, lo:hi] = _dot(xb, win_ref[:, lo:hi])
            done = max(done, upto)
            if proj_first:
                phase()

    parity = lax.rem(i, 2)

    @pl.when(parity == 0)
    def _():
        step(u_b_ref, u_a_ref)

    @pl.when(parity == 1)
    def _():
        step(u_a_ref, u_b_ref)


def _skewed_mixer_call(body, x2, batch, layer, kbd, vbd, operands, u_width, extra_scratch):
    n, d = x2.shape
    seq = n // batch
    tm = min(TM_MIX, seq)
    nt = seq // tm
    n_tiles = n // tm

    def prev(i):
        return jnp.maximum(i - 1, 0)

    in_specs = [
        pl.BlockSpec((tm, d), lambda i: (jnp.minimum(i, n_tiles - 1), 0)),
        pl.BlockSpec((tm, d), lambda i: (prev(i), 0)),
        pl.BlockSpec((1, 1) + kbd.shape[2:], lambda i: (layer, prev(i) // nt, 0, 0)),
        pl.BlockSpec((1, 1) + vbd.shape[2:], lambda i: (layer, prev(i) // nt, 0, 0)),
    ] + [_const_spec(a.shape) for a in operands]
    return pl.pallas_call(
        functools.partial(body, nt),
        grid=(n_tiles + 1,),
        in_specs=in_specs,
        out_specs=pl.BlockSpec((tm, d), lambda i: (prev(i), 0)),
        out_shape=jax.ShapeDtypeStruct((n, d), F32),
        scratch_shapes=[pltpu.VMEM((tm, u_width), F32), pltpu.VMEM((tm, u_width), F32)] + extra_scratch,
        compiler_params=pltpu.CompilerParams(
            dimension_semantics=("arbitrary",), vmem_limit_bytes=VMEM_LIMIT_BYTES),
    )(x2, x2, kbd, vbd, *operands)


def _causal_taps(buf_ref, cur_tile, w_ref, tm, tiles):
    width = w_ref.shape[0]
    outs = []
    for c in tiles:
        lanes = slice(c * LANES, (c + 1) * LANES)
        cur_c = cur_tile(c)
        buf_ref[c, pl.ds(SUBLANES, tm), :] = cur_c
        out = cur_c * w_ref[width - 1:width, lanes]
        for j in range(width - 1):
            off = SUBLANES - (width - 1) + j
            out = out + buf_ref[c, pl.ds(off, tm), :] * w_ref[j:j + 1, lanes]
        buf_ref[c, pl.ds(0, SUBLANES), :] = buf_ref[c, pl.ds(tm, SUBLANES), :]
        outs.append(out)
    return jnp.concatenate(outs, axis=1)


def _mix_conv_kernel(alpha, nt, x_ref, xp_ref, kbd_ref, vbd_ref, win_ref, convw_ref, wout_ref, g_ref,
                     b_ref, o_ref, u_ref, u_next_ref, zbuf_ref):
    tm = x_ref.shape[0]
    d_tok = convw_ref.shape[1]

    def reset():
        zbuf_ref[:, pl.ds(0, SUBLANES), :] = jnp.zeros((zbuf_ref.shape[0], SUBLANES, LANES), F32)

    def make_phases(u_ref):
        st = {}

        def z_tile(c):
            lo = d_tok + c * LANES
            return u_ref[:, lo:lo + LANES] * u_ref[:, d_tok + lo:d_tok + lo + LANES]

        def conv():
            conv = _causal_taps(zbuf_ref, z_tile, convw_ref, tm, range(d_tok // LANES))
            st["tok"] = (u_ref[:, :d_tok] * conv).astype(BF16)

        def attention():
            st["xa"] = _cross_attention(u_ref[:, 3 * d_tok:], kbd_ref[0, 0], vbd_ref[0, 0]).astype(BF16)

        def out_proj():
            st["mix"] = _dot(jnp.concatenate([st["tok"], st["xa"]], axis=1), wout_ref[...])

        def norm():
            o_ref[...] = _layer_norm(alpha * xp_ref[...] + st["mix"], g_ref[...], b_ref[...])

        return list(zip([conv, attention, out_proj, norm], CONV_PHASE_WEIGHTS))

    _skewed_body(nt, x_ref, win_ref, u_ref, u_next_ref, reset, make_phases, proj_first=False)


def _mix_conv(x2, batch, win, convw, kbd, vbd, layer, wout, g, b, alpha):
    d_tok = convw.shape[1]
    tm = min(TM_MIX, x2.shape[0] // batch)
    return _skewed_mixer_call(
        functools.partial(_mix_conv_kernel, alpha), x2, batch, layer, kbd, vbd,
        [win, convw, wout, g, b], win.shape[1],
        [pltpu.VMEM((d_tok // LANES, SUBLANES + tm, LANES), F32)])


def _head_window(h, dh):
    pair_start = (h // 2) * 2 * dh
    if h % 2 == 0:
        return pair_start, 0, dh
    return pair_start + 2 * dh - MXU_DIM, MXU_DIM - dh, 0


def _mix_mlstm_kernel(alpha, nt, x_ref, xp_ref, kbd_ref, vbd_ref, win_ref, convw_ref, bg_ref, hng_ref,
                      wout_ref, g_ref, b_ref, o_ref, u_ref, u_next_ref, cbuf_ref, c_ref, m_ref):
    tm = x_ref.shape[0]
    d_qk = convw_ref.shape[1]
    d_tok = d_qk // 2
    dh = d_tok // ML_HEADS
    length = ML_CHUNK
    width = MXU_DIM
    n_chunks = tm // length
    qk_tiles = d_qk // LANES
    heads = range(ML_HEADS)
    pairs = [(c, h) for c in range(n_chunks) for h in heads]

    def reset():
        cbuf_ref[:, pl.ds(0, SUBLANES), :] = jnp.zeros((cbuf_ref.shape[0], SUBLANES, LANES), F32)
        c_ref[...] = jnp.zeros(c_ref.shape, F32)
        m_ref[...] = jnp.zeros(m_ref.shape, F32)

    def make_phases(u_ref):
        st = {}
        window = [_head_window(h, dh) for h in heads]

        def rows(c):
            return slice(c * length, (c + 1) * length)

        def u_tile(c):
            return u_ref[:, c * LANES:(c + 1) * LANES]

        def conv(part):
            def phase():
                lo = part * qk_tiles // 4
                a = _causal_taps(cbuf_ref, u_tile, convw_ref, tm, range(lo, lo + qk_tiles // 4))
                a = a * _sigmoid(a)
                st["qk", part] = a.astype(BF16) if part < 2 else a * (dh ** -0.5)
                if part == 1:
                    st["q"] = jnp.concatenate([st.pop(("qk", 0)), st.pop(("qk", 1))], axis=1)
                if part == 3:
                    st["k"] = jnp.concatenate([st.pop(("qk", 2)), st.pop(("qk", 3))], axis=1)
            return phase

        def gate_math():
            gates = u_ref[:, d_qk + 2 * d_tok:d_qk + 2 * d_tok + GATE_PAD] + bg_ref[...]
            glane = lax.broadcasted_iota(jnp.int32, gates.shape, 1)
            log_sig = jnp.minimum(gates, 0.0) - jnp.log(1.0 + jnp.exp(-jnp.abs(gates)))
            gl = jnp.where(glane < ML_HEADS, gates, log_sig)
            row = lax.broadcasted_iota(jnp.int32, (length, length), 0)
            col = lax.broadcasted_iota(jnp.int32, (length, length), 1)
            st["tri"] = row >= col
            wlane = lax.broadcasted_iota(jnp.int32, (length, width), 1)
            st["valid"] = [(wlane >= first) & (wlane < first + dh) for _, first, _ in window]
            st["spare"] = [wlane == spare for _, _, spare in window]
            tri_b = st["tri"].astype(BF16)
            gcat = jnp.concatenate([gl[rows(c)] for c in range(n_chunks)], axis=1)
            cums, rest = None, gcat
            for _ in range(3):
                term = rest.astype(BF16)
                rest = rest - term.astype(F32)
                part = _dot(tri_b, term)
                cums = part if cums is None else cums + part
            for c in range(n_chunks):
                gc = gcat[:, c * GATE_PAD:(c + 1) * GATE_PAD]
                cum = cums[:, c * GATE_PAD:(c + 1) * GATE_PAD]
                gct, cumt = gc.T, cum.T
                for h in heads:
                    st["icol", c, h] = gc[:, h:h + 1]
                    st["bcol", c, h] = cum[:, ML_HEADS + h:ML_HEADS + h + 1]
                    st["irow", c, h] = gct[h:h + 1, :]
                    st["brow", c, h] = cumt[ML_HEADS + h:ML_HEADS + h + 1, :]
                    st["blast", c, h] = cum[length - 1:length, ML_HEADS + h:ML_HEADS + h + 1]

        def decay_weights():
            for c, h in pairs:
                st["logd", c, h] = jnp.where(
                    st["tri"], st["bcol", c, h] - st["brow", c, h] + st["irow", c, h], NEG_BIG)
            for c, h in pairs:
                st["a", c, h] = jnp.max(st["logd", c, h], axis=-1, keepdims=True)
            for c, h in pairs:
                st["p", c, h] = jnp.exp(st.pop(("logd", c, h)) - st["a", c, h])

        def scores():
            for c, h in pairs:
                start = window[h][0]
                st["qw", c, h] = st["q"][rows(c), start:start + width]
                st["kw", c, h] = jnp.where(st["valid"][h], st["k"][rows(c), start:start + width],
                                           0.0).astype(BF16)
            for c, h in pairs:
                s = _dot_nt(st["qw", c, h], st["kw", c, h])
                st["sc", c, h] = (s * st.pop(("p", c, h))).astype(BF16)

        def intra():
            for c, h in pairs:
                start = window[h][0]
                vw = u_ref[rows(c), d_qk + start:d_qk + start + width]
                st["vaug", c, h] = jnp.where(st["spare"][h], 1.0, vw)
            for c, h in pairs:
                st["intra", c, h] = _dot(st.pop(("sc", c, h)), st["vaug", c, h].astype(BF16))

        def recur(c):
            def combine():
                m = [m_ref[h, 0:1, 0:1] for h in heads]
                c_st = [c_ref[h] for h in heads]
                inter = [_dot(st["qw", c, h], c_st[h].astype(BF16)) for h in heads]
                log_inter = [st["bcol", c, h] + m[h] for h in heads]
                m_t = [jnp.maximum(log_inter[h], st["a", c, h]) for h in heads]
                num = [jnp.exp(st["a", c, h] - m_t[h]) * st.pop(("intra", c, h))
                       + jnp.exp(log_inter[h] - m_t[h]) * inter[h] for h in heads]
                den = [jnp.sum(jnp.where(st["spare"][h], num[h], 0.0), axis=-1, keepdims=True)
                       for h in heads]
                for h in heads:
                    st["m", h], st["c_st", h] = m[h], c_st[h]
                    st["hw", h] = num[h] / jnp.maximum(jnp.abs(den[h]), jnp.exp(-m_t[h]))

            def update():
                m = [st.pop(("m", h)) for h in heads]
                blast = [st["blast", c, h] for h in heads]
                log_w_row = [blast[h] - st["brow", c, h] + st["irow", c, h] for h in heads]
                m_new = [jnp.maximum(blast[h] + m[h], jnp.max(log_w_row[h], axis=-1, keepdims=True))
                         for h in heads]
                w_k = [jnp.exp(blast[h] - st["bcol", c, h] + st["icol", c, h] - m_new[h])
                       for h in heads]
                upd = [_dot_tn(st["kw", c, h], (w_k[h] * st.pop(("vaug", c, h))).astype(BF16))
                       for h in heads]
                for h in heads:
                    c_ref[h] = jnp.exp(blast[h] + m[h] - m_new[h]) * st.pop(("c_st", h)) + upd[h]
                    m_ref[h] = jnp.broadcast_to(m_new[h], m_ref.shape[1:])

            def head_norm():
                hw = [st.pop(("hw", h)) for h in heads]
                mu = [jnp.sum(jnp.where(st["valid"][h], hw[h], 0.0), axis=-1, keepdims=True) / dh
                      for h in heads]
                d = [jnp.where(st["valid"][h], hw[h] - mu[h], 0.0) for h in heads]
                var = [jnp.sum(d[h] * d[h], axis=-1, keepdims=True) / dh for h in heads]
                for h in heads:
                    st["win", c, h] = d[h] * lax.rsqrt(var[h] + LN_EPS)

            return [(combine, 1), (update, 0), (head_norm, 1)]

        def tail_gate():
            hn_rows = []
            for c in range(n_chunks):
                pieces = []
                for p in range(ML_HEADS // 2):
                    even, odd = st["win", c, 2 * p], st["win", c, 2 * p + 1]
                    pieces += [even[:, :LANES], even[:, LANES:] + odd[:, :LANES], odd[:, LANES:]]
                hn_rows.append(jnp.concatenate(pieces, axis=1))
            hn = hn_rows[0] if n_chunks == 1 else jnp.concatenate(hn_rows, axis=0)
            o_pre = u_ref[:, d_qk + d_tok:d_qk + 2 * d_tok]
            st["tok"] = (_sigmoid(o_pre) * (hn * hng_ref[...])).astype(BF16)

        def tail_attention():
            q_mem = u_ref[:, d_qk + 2 * d_tok + GATE_PAD:]
            xa = _cross_attention(q_mem, kbd_ref[0, 0], vbd_ref[0, 0])
            cat = jnp.concatenate([st["tok"], xa.astype(BF16)], axis=1)
            st["mix"] = _dot(cat, wout_ref[...])

        def norm():
            o_ref[...] = _layer_norm(alpha * xp_ref[...] + st["mix"], g_ref[...], b_ref[...])

        recurs = [p for c in range(n_chunks) for p, _ in recur(c)]
        phases = ([conv(part) for part in range(4)] + [gate_math, decay_weights, scores, intra]
                  + recurs + [tail_gate, tail_attention, norm])
        weights = (MLSTM_PHASE_WEIGHTS[:8] + MLSTM_PHASE_WEIGHTS[8:11] * n_chunks
                   + MLSTM_PHASE_WEIGHTS[11:])
        return list(zip(phases, weights))

    _skewed_body(nt, x_ref, win_ref, u_ref, u_next_ref, reset, make_phases, proj_first=True)


def _mix_mlstm(x2, batch, win, convw, bg, hng, kbd, vbd, layer, wout, g, b, alpha):
    d_qk = convw.shape[1]
    tm = min(TM_MIX, x2.shape[0] // batch)
    return _skewed_mixer_call(
        functools.partial(_mix_mlstm_kernel, alpha), x2, batch, layer, kbd, vbd,
        [win, convw, bg, hng, wout, g, b], win.shape[1],
        [pltpu.VMEM((d_qk // LANES, SUBLANES + tm, LANES), F32),
         pltpu.VMEM((ML_HEADS, MXU_DIM, MXU_DIM), F32),
         pltpu.VMEM((ML_HEADS, SUBLANES, LANES), F32)])


def _prep_mlstm(w, b_gates, head_norm_g, d_tok):
    n_gate = 2 * ML_HEADS
    gate_w = jnp.pad(w[:, 4 * d_tok:4 * d_tok + n_gate], ((0, 0), (0, GATE_PAD - n_gate)))
    win = jnp.concatenate([w[:, :4 * d_tok], gate_w, w[:, 4 * d_tok + n_gate:]], axis=1).astype(BF16)
    bg = jnp.pad(b_gates, (0, GATE_PAD - n_gate))[None, :]
    return win, bg, head_norm_g.reshape(1, d_tok)


def kernel(x, mem, ln_g, ln_b, ffn_w_gate, ffn_w_up, ffn_w_down, w_kv_mem, w_out, w_in_conv, conv_w,
           w_in_mlstm, b_gates, qk_conv_w, head_norm_g):
    batch, seq, d = x.shape
    depth = ffn_w_gate.shape[0]
    d_xa = w_kv_mem.shape[-1] // 2
    d_tok = conv_w.shape[-1]
    n_mixers = 2
    alpha = (2.0 * depth) ** 0.25
    assert seq % ML_CHUNK == 0 and d_tok % ML_HEADS == 0
    assert 2 * (d_tok // ML_HEADS) % LANES == 0 and d_tok // ML_HEADS <= MXU_DIM

    ffn_params = (ffn_w_gate, ffn_w_up, ffn_w_down)
    ffn_w = tuple(w[0, 0].astype(BF16) for w in ffn_params)
    ffn_order = [(l, pos) for l in range(depth) for pos in range(2)]

    def ffn(x2, g, b):
        nonlocal ffn_w
        ffn_order.pop(0)
        nxt = (*ffn_params, *ffn_order[0]) if ffn_order else None
        x2, ffn_w = _ffn_ln(x2, ffn_w, g, b, alpha, nxt)
        return x2

    wo = w_out.astype(BF16)
    wkt = jnp.swapaxes(w_kv_mem[:, :, :d_xa], 1, 2).astype(BF16)
    wv = w_kv_mem[:, :, d_xa:].astype(BF16)
    kbd, vbd = _mem_kv(mem, wkt, wv)

    x2 = x.reshape(batch * seq, d)
    for l in range(depth):
        g = ln_g[l][:, None, :]
        b = ln_b[l][:, None, :]
        x2 = ffn(x2, g[0], b[0])
        j = l // n_mixers
        if l % n_mixers == 0:
            x2 = _mix_conv(x2, batch, w_in_conv[j].astype(BF16), conv_w[j], kbd, vbd, l, wo[l],
                           g[1], b[1], alpha)
        else:
            win, bg, hng = _prep_mlstm(w_in_mlstm[j], b_gates[j], head_norm_g[j], d_tok)
            x2 = _mix_mlstm(x2, batch, win, qk_conv_w[j], bg, hng, kbd, vbd, l, wo[l],
                            g[1], b[1], alpha)
        x2 = ffn(x2, g[2], b[2])
    return x2.reshape(batch, seq, d)
```

```python
import functools

import jax
import jax.numpy as jnp
from jax import lax
from jax.experimental import pallas as pl
from jax.experimental.pallas import tpu as pltpu

F32 = jnp.float32
BF16 = jnp.bfloat16

XA_HEADS = 4
CONV_WIDTH = 3
QK_CONV_WIDTH = 4
ML_HEADS = 4
LN_EPS = 1e-5

LANES = 128
SUBLANES = 8
MXU_DIM = 256
VMEM_LIMIT_BYTES = 56 * 1024 * 1024
FFN_FLAGS = None
MIXER_FLAGS = None

TM_FFN = 1024
FFN_SUB_ROWS = (768, 256)
FFN_FF_CHUNKS = (1536, 1280)
TM_MIX = 512
PROJ_CHUNK = 256
CONV_PHASE_WEIGHTS = (1, 3, 2, 4)
MLSTM_PHASE_WEIGHTS = (1, 1, 1, 1, 1, 1, 1, 0, 1, 0, 1, 1, 1, 1)
ML_CHUNK = MXU_DIM
GATE_PAD = LANES
NEG_BIG = -1e30


def _layer_norm(r, g, b):
    mu = jnp.mean(r, axis=-1, keepdims=True)
    d = r - mu
    var = jnp.mean(d * d, axis=-1, keepdims=True)
    return d * lax.rsqrt(var + LN_EPS) * g + b


def _sigmoid(x):
    return 1.0 / (1.0 + jnp.exp(-x))


def _dot(a, b):
    return jnp.dot(a, b, preferred_element_type=F32)


def _dot_nt(a, b):
    return lax.dot_general(a, b, (((1,), (1,)), ((), ())), preferred_element_type=F32)


def _dot_tn(a, b):
    return lax.dot_general(a, b, (((0,), (0,)), ((), ())), preferred_element_type=F32)


def _const_spec(shape):
    nd = len(shape)
    return pl.BlockSpec(shape, lambda *_: (0,) * nd)


def _ffn_ln_kernel(alpha, sub_rows, x_ref, wg_ref, wu_ref, wd_ref, g_ref, b_ref, *rest):
    o_ref = rest[len(rest) // 2]
    xb = x_ref[...].astype(BF16)
    hs, lo = [], 0
    for width in FFN_FF_CHUNKS:
        cols = slice(lo, lo + width)
        gate = _dot(xb, wg_ref[:, cols])
        up = _dot(xb, wu_ref[:, cols])
        hs.append((gate * _sigmoid(gate) * up).astype(BF16))
        lo += width
    h = jnp.concatenate(hs, axis=1)
    for src_ref, dst_ref in zip(rest[:len(rest) // 2], rest[len(rest) // 2 + 1:]):
        dst_ref[...] = src_ref[...].astype(BF16)
    pending, start = None, 0
    for ts in sub_rows:
        y = _dot(h[start:start + ts], wd_ref[...])
        if pending is not None:
            prows, r = pending
            o_ref[prows, :] = _layer_norm(r, g_ref[...], b_ref[...])
        rows = pl.ds(start, ts)
        pending = (rows, alpha * x_ref[rows, :] + 0.5 * y)
        start += ts
    prows, r = pending
    o_ref[prows, :] = _layer_norm(r, g_ref[...], b_ref[...])


def _row_pieces(rows, n_steps):
    return max(p for p in range(1, n_steps + 1)
               if n_steps % p == 0 and rows % p == 0 and (rows // p) % (2 * SUBLANES) == 0)


def _ffn_ln(x2, weights, g, b, alpha, next_weights=None):
    n, d = x2.shape
    d_ff = weights[0].shape[-1]
    assert sum(FFN_FF_CHUNKS) == d_ff and all(w % MXU_DIM == 0 for w in FFN_FF_CHUNKS)
    tm = min(TM_FFN, n)
    n_steps = n // tm
    sub_rows = FFN_SUB_ROWS if sum(FFN_SUB_ROWS) == tm else (tm,)
    in_specs = [pl.BlockSpec((tm, d), lambda i: (i, 0))]
    in_specs += [_const_spec(w.shape) for w in weights] + [_const_spec((1, d)), _const_spec((1, d))]
    out_specs = [pl.BlockSpec((tm, d), lambda i: (i, 0))]
    out_shape = [jax.ShapeDtypeStruct((n, d), F32)]
    operands = [x2, *weights, g, b]
    if next_weights is not None:
        *params, layer, pos = next_weights
        for w in params:
            rows, cols = w.shape[2:]
            pieces = _row_pieces(rows, n_steps)

            def piece(i, pieces=pieces):
                return i * pieces // n_steps

            in_specs.append(pl.BlockSpec((None, None, rows // pieces, cols),
                                         lambda i, piece=piece: (layer, pos, piece(i), 0)))
            out_specs.append(pl.BlockSpec((rows // pieces, cols), lambda i, piece=piece: (piece(i), 0)))
            out_shape.append(jax.ShapeDtypeStruct((rows, cols), BF16))
            operands.append(w)
    out = pl.pallas_call(
        functools.partial(_ffn_ln_kernel, alpha, sub_rows),
        grid=(n_steps,),
        in_specs=in_specs,
        out_specs=out_specs,
        out_shape=out_shape,
        compiler_params=pltpu.CompilerParams(
            dimension_semantics=("arbitrary",), vmem_limit_bytes=VMEM_LIMIT_BYTES, flags=FFN_FLAGS),
    )(*operands)
    return out[0], tuple(out[1:])


def _mem_kv_kernel(scale, mem_ref, wkt_ref, wv_ref, kbd_ref, vbd_ref):
    n_batch, n_mem, _ = mem_ref.shape
    d_xa = wv_ref.shape[-1]
    hd = d_xa // XA_HEADS
    k_shape, v_shape = kbd_ref.shape[2:], vbd_ref.shape[2:]
    k_mask = (lax.broadcasted_iota(jnp.int32, k_shape, 0) // hd
              == lax.broadcasted_iota(jnp.int32, k_shape, 1) // n_mem)
    v_mask = (lax.broadcasted_iota(jnp.int32, v_shape, 0) // n_mem
              == lax.broadcasted_iota(jnp.int32, v_shape, 1) // hd)
    for i in range(n_batch):
        memb = mem_ref[i].astype(BF16)
        kt = _dot_nt(wkt_ref[0], memb) * scale
        v = _dot(memb, wv_ref[0])
        kt4 = jnp.concatenate([kt] * XA_HEADS, axis=1)
        kbd_ref[0, i] = jnp.where(k_mask, kt4, 0.0).astype(BF16)
        v4 = jnp.concatenate([v] * XA_HEADS, axis=0)
        vbd_ref[0, i] = jnp.where(v_mask, v4, 0.0).astype(BF16)


def _mem_kv(mem, wkt, wv):
    b, n_mem, d = mem.shape
    depth, d_xa, _ = wkt.shape
    hd = d_xa // XA_HEADS
    return pl.pallas_call(
        functools.partial(_mem_kv_kernel, hd ** -0.5),
        grid=(depth,),
        in_specs=[
            pl.BlockSpec((b, n_mem, d), lambda l: (0, 0, 0)),
            pl.BlockSpec((1, d_xa, d), lambda l: (l, 0, 0)),
            pl.BlockSpec((1, d, d_xa), lambda l: (l, 0, 0)),
        ],
        out_specs=[
            pl.BlockSpec((1, b, d_xa, XA_HEADS * n_mem), lambda l: (l, 0, 0, 0)),
            pl.BlockSpec((1, b, XA_HEADS * n_mem, d_xa), lambda l: (l, 0, 0, 0)),
        ],
        out_shape=[
            jax.ShapeDtypeStruct((depth, b, d_xa, XA_HEADS * n_mem), BF16),
            jax.ShapeDtypeStruct((depth, b, XA_HEADS * n_mem, d_xa), BF16),
        ],
        compiler_params=pltpu.CompilerParams(
            dimension_semantics=("arbitrary",), vmem_limit_bytes=VMEM_LIMIT_BYTES),
    )(mem, wkt, wv)


def _cross_attention(q, kbd, vbd):
    tm, d_xa = q.shape
    n_mem = kbd.shape[1] // XA_HEADS
    hd = d_xa // XA_HEADS
    s = _dot(q.astype(BF16), kbd)
    heads = range(XA_HEADS)
    sh = [s[:, h * n_mem:(h + 1) * n_mem] for h in heads]
    mx = [jnp.max(sh[h], axis=-1, keepdims=True) for h in heads]
    p = [jnp.exp(sh[h] - mx[h]) for h in heads]
    ls = [jnp.sum(p[h], axis=-1, keepdims=True) for h in heads]
    o = _dot(jnp.concatenate([p[h].astype(BF16) for h in heads], axis=1), vbd)
    lane_head = lax.broadcasted_iota(jnp.int32, (tm, d_xa), 1) // hd
    l = jnp.broadcast_to(ls[XA_HEADS - 1], (tm, d_xa))
    for h in range(XA_HEADS - 2, -1, -1):
        l = jnp.where(lane_head == h, ls[h], l)
    return o / l


def _skewed_body(nt, x_ref, win_ref, u_a_ref, u_b_ref, reset_fn, make_phases, proj_first):
    i = pl.program_id(0)

    @pl.when(i == 0)
    def _():
        u_b_ref[...] = jnp.zeros(u_b_ref.shape, F32)

    @pl.when(lax.rem(jnp.maximum(i - 1, 0), nt) == 0)
    def _():
        reset_fn()

    width = win_ref.shape[1]
    chunks = [(lo, min(lo + PROJ_CHUNK, width)) for lo in range(0, width, PROJ_CHUNK)]

    def step(u_read_ref, u_write_ref):
        xb = x_ref[...].astype(BF16)
        phases = make_phases(u_read_ref)
        total = sum(w for _, w in phases)
        done, acc = 0, 0
        for phase, weight in phases:
            acc += weight
            upto = -(-acc * len(chunks) // total)
            if not proj_first:
                phase()
            for lo, hi in chunks[done:upto]:
                u_write_ref[:, lo:hi] = _dot(xb, win_ref[:, lo:hi])
            done = max(done, upto)
            if proj_first:
                phase()

    parity = lax.rem(i, 2)

    @pl.when(parity == 0)
    def _():
        step(u_b_ref, u_a_ref)

    @pl.when(parity == 1)
    def _():
        step(u_a_ref, u_b_ref)


def _skewed_mixer_call(body, x2, batch, layer, kbd, vbd, operands, u_width, extra_scratch):
    n, d = x2.shape
    seq = n // batch
    tm = min(TM_MIX, seq)
    nt = seq // tm
    n_tiles = n // tm

    def prev(i):
        return jnp.maximum(i - 1, 0)

    in_specs = [
        pl.BlockSpec((tm, d), lambda i: (jnp.minimum(i, n_tiles - 1), 0)),
        pl.BlockSpec((tm, d), lambda i: (prev(i), 0)),
        pl.BlockSpec((1, 1) + kbd.shape[2:], lambda i: (layer, prev(i) // nt, 0, 0)),
        pl.BlockSpec((1, 1) + vbd.shape[2:], lambda i: (layer, prev(i) // nt, 0, 0)),
    ] + [_const_spec(a.shape) for a in operands]
    return pl.pallas_call(
        functools.partial(body, nt),
        grid=(n_tiles + 1,),
        in_specs=in_specs,
        out_specs=pl.BlockSpec((tm, d), lambda i: (prev(i), 0)),
        out_shape=jax.ShapeDtypeStruct((n, d), F32),
        scratch_shapes=[pltpu.VMEM((tm, u_width), F32), pltpu.VMEM((tm, u_width), F32)] + extra_scratch,
        compiler_params=pltpu.CompilerParams(
            dimension_semantics=("arbitrary",), vmem_limit_bytes=VMEM_LIMIT_BYTES, flags=MIXER_FLAGS),
    )(x2, x2, kbd, vbd, *operands)


def _causal_taps(buf_ref, cur_tile, w_ref, tm, tiles):
    width = w_ref.shape[0]
    outs = []
    for c in tiles:
        lanes = slice(c * LANES, (c + 1) * LANES)
        cur_c = cur_tile(c)
        buf_ref[c, pl.ds(SUBLANES, tm), :] = cur_c
        out = cur_c * w_ref[width - 1:width, lanes]
        for j in range(width - 1):
            off = SUBLANES - (width - 1) + j
            out = out + buf_ref[c, pl.ds(off, tm), :] * w_ref[j:j + 1, lanes]
        buf_ref[c, pl.ds(0, SUBLANES), :] = buf_ref[c, pl.ds(tm, SUBLANES), :]
        outs.append(out)
    return jnp.concatenate(outs, axis=1)


def _mix_conv_kernel(alpha, nt, x_ref, xp_ref, kbd_ref, vbd_ref, win_ref, convw_ref, wout_ref, g_ref,
                     b_ref, o_ref, u_ref, u_next_ref, zbuf_ref):
    tm = x_ref.shape[0]
    d_tok = convw_ref.shape[1]

    def reset():
        zbuf_ref[:, pl.ds(0, SUBLANES), :] = jnp.zeros((zbuf_ref.shape[0], SUBLANES, LANES), F32)

    def make_phases(u_ref):
        st = {}

        def z_tile(c):
            lo = d_tok + c * LANES
            return u_ref[:, lo:lo + LANES] * u_ref[:, d_tok + lo:d_tok + lo + LANES]

        def conv():
            conv = _causal_taps(zbuf_ref, z_tile, convw_ref, tm, range(d_tok // LANES))
            st["tok"] = (u_ref[:, :d_tok] * conv).astype(BF16)

        def attention():
            st["xa"] = _cross_attention(u_ref[:, 3 * d_tok:], kbd_ref[0, 0], vbd_ref[0, 0]).astype(BF16)

        def out_proj():
            st["mix"] = _dot(jnp.concatenate([st["tok"], st["xa"]], axis=1), wout_ref[...])

        def norm():
            o_ref[...] = _layer_norm(alpha * xp_ref[...] + st["mix"], g_ref[...], b_ref[...])

        return list(zip([conv, attention, out_proj, norm], CONV_PHASE_WEIGHTS))

    _skewed_body(nt, x_ref, win_ref, u_ref, u_next_ref, reset, make_phases, proj_first=False)


def _mix_conv(x2, batch, win, convw, kbd, vbd, layer, wout, g, b, alpha):
    d_tok = convw.shape[1]
    tm = min(TM_MIX, x2.shape[0] // batch)
    return _skewed_mixer_call(
        functools.partial(_mix_conv_kernel, alpha), x2, batch, layer, kbd, vbd,
        [win, convw, wout, g, b], win.shape[1],
        [pltpu.VMEM((d_tok // LANES, SUBLANES + tm, LANES), F32)])


def _head_window(h, dh):
    pair_start = (h // 2) * 2 * dh
    if h % 2 == 0:
        return pair_start, 0, dh
    return pair_start + 2 * dh - MXU_DIM, MXU_DIM - dh, 0


def _mix_mlstm_kernel(alpha, nt, x_ref, xp_ref, kbd_ref, vbd_ref, win_ref, convw_ref, bg_ref, hng_ref,
                      wout_ref, g_ref, b_ref, o_ref, u_ref, u_next_ref, cbuf_ref, c_ref, m_ref):
    tm = x_ref.shape[0]
    d_qk = convw_ref.shape[1]
    d_tok = d_qk // 2
    dh = d_tok // ML_HEADS
    length = ML_CHUNK
    width = MXU_DIM
    n_chunks = tm // length
    qk_tiles = d_qk // LANES
    heads = range(ML_HEADS)
    pairs = [(c, h) for c in range(n_chunks) for h in heads]

    def reset():
        cbuf_ref[:, pl.ds(0, SUBLANES), :] = jnp.zeros((cbuf_ref.shape[0], SUBLANES, LANES), F32)
        c_ref[...] = jnp.zeros(c_ref.shape, F32)
        m_ref[...] = jnp.zeros(m_ref.shape, F32)

    def make_phases(u_ref):
        st = {}
        window = [_head_window(h, dh) for h in heads]

        def rows(c):
            return slice(c * length, (c + 1) * length)

        def u_tile(c):
            return u_ref[:, c * LANES:(c + 1) * LANES]

        def conv(part):
            def phase():
                lo = part * qk_tiles // 4
                a = _causal_taps(cbuf_ref, u_tile, convw_ref, tm, range(lo, lo + qk_tiles // 4))
                a = a * _sigmoid(a)
                st["qk", part] = a.astype(BF16) if part < 2 else a * (dh ** -0.5)
                if part == 1:
                    st["q"] = jnp.concatenate([st.pop(("qk", 0)), st.pop(("qk", 1))], axis=1)
                if part == 3:
                    st["k"] = jnp.concatenate([st.pop(("qk", 2)), st.pop(("qk", 3))], axis=1)
            return phase

        def gate_math():
            gates = u_ref[:, d_qk + 2 * d_tok:d_qk + 2 * d_tok + GATE_PAD] + bg_ref[...]
            glane = lax.broadcasted_iota(jnp.int32, gates.shape, 1)
            log_sig = jnp.minimum(gates, 0.0) - jnp.log(1.0 + jnp.exp(-jnp.abs(gates)))
            gl = jnp.where(glane < ML_HEADS, gates, log_sig)
            row = lax.broadcasted_iota(jnp.int32, (length, length), 0)
            col = lax.broadcasted_iota(jnp.int32, (length, length), 1)
            st["tri"] = row >= col
            wlane = lax.broadcasted_iota(jnp.int32, (length, width), 1)
            st["valid"] = [(wlane >= first) & (wlane < first + dh) for _, first, _ in window]
            st["spare"] = [wlane == spare for _, _, spare in window]
            tri_b = st["tri"].astype(BF16)
            gcat = jnp.concatenate([gl[rows(c)] for c in range(n_chunks)], axis=1)
            cums, rest = None, gcat
            for _ in range(3):
                term = rest.astype(BF16)
                rest = rest - term.astype(F32)
                part = _dot(tri_b, term)
                cums = part if cums is None else cums + part
            for c in range(n_chunks):
                gc = gcat[:, c * GATE_PAD:(c + 1) * GATE_PAD]
                cum = cums[:, c * GATE_PAD:(c + 1) * GATE_PAD]
                gct, cumt = gc.T, cum.T
                for h in heads:
                    st["icol", c, h] = gc[:, h:h + 1]
                    st["bcol", c, h] = cum[:, ML_HEADS + h:ML_HEADS + h + 1]
                    st["irow", c, h] = gct[h:h + 1, :]
                    st["brow", c, h] = cumt[ML_HEADS + h:ML_HEADS + h + 1, :]
                    st["blast", c, h] = cum[length - 1:length, ML_HEADS + h:ML_HEADS + h + 1]

        def decay_weights():
            for c, h in pairs:
                st["logd", c, h] = jnp.where(
                    st["tri"], st["bcol", c, h] - st["brow", c, h] + st["irow", c, h], NEG_BIG)
            for c, h in pairs:
                st["a", c, h] = jnp.max(st["logd", c, h], axis=-1, keepdims=True)
            for c, h in pairs:
                st["p", c, h] = jnp.exp(st.pop(("logd", c, h)) - st["a", c, h])

        def scores():
            for c, h in pairs:
                start = window[h][0]
                st["qw", c, h] = st["q"][rows(c), start:start + width]
                st["kw", c, h] = jnp.where(st["valid"][h], st["k"][rows(c), start:start + width],
                                           0.0).astype(BF16)
            for c, h in pairs:
                s = _dot_nt(st["qw", c, h], st["kw", c, h])
                st["sc", c, h] = (s * st.pop(("p", c, h))).astype(BF16)

        def intra():
            for c, h in pairs:
                start = window[h][0]
                vw = u_ref[rows(c), d_qk + start:d_qk + start + width]
                st["vaug", c, h] = jnp.where(st["spare"][h], 1.0, vw)
            for c, h in pairs:
                st["intra", c, h] = _dot(st.pop(("sc", c, h)), st["vaug", c, h].astype(BF16))

        def recur(c):
            def combine():
                m = [m_ref[h, 0:1, 0:1] for h in heads]
                c_st = [c_ref[h] for h in heads]
                inter = [_dot(st["qw", c, h], c_st[h].astype(BF16)) for h in heads]
                log_inter = [st["bcol", c, h] + m[h] for h in heads]
                m_t = [jnp.maximum(log_inter[h], st["a", c, h]) for h in heads]
                num = [jnp.exp(st["a", c, h] - m_t[h]) * st.pop(("intra", c, h))
                       + jnp.exp(log_inter[h] - m_t[h]) * inter[h] for h in heads]
                den = [jnp.sum(jnp.where(st["spare"][h], num[h], 0.0), axis=-1, keepdims=True)
                       for h in heads]
                for h in heads:
                    st["m", h], st["c_st", h] = m[h], c_st[h]
                    st["hw", h] = num[h] / jnp.maximum(jnp.abs(den[h]), jnp.exp(-m_t[h]))

            def update():
                m = [st.pop(("m", h)) for h in heads]
                blast = [st["blast", c, h] for h in heads]
                log_w_row = [blast[h] - st["brow", c, h] + st["irow", c, h] for h in heads]
                m_new = [jnp.maximum(blast[h] + m[h], jnp.max(log_w_row[h], axis=-1, keepdims=True))
                         for h in heads]
                w_k = [jnp.exp(blast[h] - st["bcol", c, h] + st["icol", c, h] - m_new[h])
                       for h in heads]
                upd = [_dot_tn(st["kw", c, h], (w_k[h] * st.pop(("vaug", c, h))).astype(BF16))
                       for h in heads]
                for h in heads:
                    c_ref[h] = jnp.exp(blast[h] + m[h] - m_new[h]) * st.pop(("c_st", h)) + upd[h]
                    m_ref[h] = jnp.broadcast_to(m_new[h], m_ref.shape[1:])

            def head_norm():
                hw = [st.pop(("hw", h)) for h in heads]
                mu = [jnp.sum(jnp.where(st["valid"][h], hw[h], 0.0), axis=-1, keepdims=True) / dh
                      for h in heads]
                d = [jnp.where(st["valid"][h], hw[h] - mu[h], 0.0) for h in heads]
                var = [jnp.sum(d[h] * d[h], axis=-1, keepdims=True) / dh for h in heads]
                for h in heads:
                    st["win", c, h] = d[h] * lax.rsqrt(var[h] + LN_EPS)

            return [(combine, 1), (update, 0), (head_norm, 1)]

        def tail_gate():
            hn_rows = []
            for c in range(n_chunks):
                pieces = []
                for p in range(ML_HEADS // 2):
                    even, odd = st["win", c, 2 * p], st["win", c, 2 * p + 1]
                    pieces += [even[:, :LANES], even[:, LANES:] + odd[:, :LANES], odd[:, LANES:]]
                hn_rows.append(jnp.concatenate(pieces, axis=1))
            hn = hn_rows[0] if n_chunks == 1 else jnp.concatenate(hn_rows, axis=0)
            o_pre = u_ref[:, d_qk + d_tok:d_qk + 2 * d_tok]
            st["tok"] = (_sigmoid(o_pre) * (hn * hng_ref[...])).astype(BF16)

        def tail_attention():
            q_mem = u_ref[:, d_qk + 2 * d_tok + GATE_PAD:]
            xa = _cross_attention(q_mem, kbd_ref[0, 0], vbd_ref[0, 0])
            cat = jnp.concatenate([st["tok"], xa.astype(BF16)], axis=1)
            st["mix"] = _dot(cat, wout_ref[...])

        def norm():
            o_ref[...] = _layer_norm(alpha * xp_ref[...] + st["mix"], g_ref[...], b_ref[...])

        recurs = [p for c in range(n_chunks) for p, _ in recur(c)]
        phases = ([conv(part) for part in range(4)] + [gate_math, decay_weights, scores, intra]
                  + recurs + [tail_gate, tail_attention, norm])
        weights = (MLSTM_PHASE_WEIGHTS[:8] + MLSTM_PHASE_WEIGHTS[8:11] * n_chunks
                   + MLSTM_PHASE_WEIGHTS[11:])
        return list(zip(phases, weights))

    _skewed_body(nt, x_ref, win_ref, u_ref, u_next_ref, reset, make_phases, proj_first=True)


def _mix_mlstm(x2, batch, win, convw, bg, hng, kbd, vbd, layer, wout, g, b, alpha):
    d_qk = convw.shape[1]
    tm = min(TM_MIX, x2.shape[0] // batch)
    return _skewed_mixer_call(
        functools.partial(_mix_mlstm_kernel, alpha), x2, batch, layer, kbd, vbd,
        [win, convw, bg, hng, wout, g, b], win.shape[1],
        [pltpu.VMEM((d_qk // LANES, SUBLANES + tm, LANES), F32),
         pltpu.VMEM((ML_HEADS, MXU_DIM, MXU_DIM), F32),
         pltpu.VMEM((ML_HEADS, SUBLANES, LANES), F32)])


def _prep_mlstm(w, b_gates, head_norm_g, d_tok):
    n_gate = 2 * ML_HEADS
    gate_w = jnp.pad(w[:, 4 * d_tok:4 * d_tok + n_gate], ((0, 0), (0, GATE_PAD - n_gate)))
    win = jnp.concatenate([w[:, :4 * d_tok], gate_w, w[:, 4 * d_tok + n_gate:]], axis=1).astype(BF16)
    bg = jnp.pad(b_gates, (0, GATE_PAD - n_gate))[None, :]
    return win, bg, head_norm_g.reshape(1, d_tok)


def kernel(x, mem, ln_g, ln_b, ffn_w_gate, ffn_w_up, ffn_w_down, w_kv_mem, w_out, w_in_conv, conv_w,
           w_in_mlstm, b_gates, qk_conv_w, head_norm_g):
    batch, seq, d = x.shape
    depth = ffn_w_gate.shape[0]
    d_xa = w_kv_mem.shape[-1] // 2
    d_tok = conv_w.shape[-1]
    n_mixers = 2
    alpha = (2.0 * depth) ** 0.25
    assert seq % ML_CHUNK == 0 and d_tok % ML_HEADS == 0
    assert 2 * (d_tok // ML_HEADS) % LANES == 0 and d_tok // ML_HEADS <= MXU_DIM

    ffn_params = (ffn_w_gate, ffn_w_up, ffn_w_down)
    ffn_w = tuple(w[0, 0].astype(BF16) for w in ffn_params)
    ffn_order = [(l, pos) for l in range(depth) for pos in range(2)]

    def ffn(x2, g, b):
        nonlocal ffn_w
        ffn_order.pop(0)
        nxt = (*ffn_params, *ffn_order[0]) if ffn_order else None
        x2, ffn_w = _ffn_ln(x2, ffn_w, g, b, alpha, nxt)
        return x2

    wo = w_out.astype(BF16)
    wkt = jnp.swapaxes(w_kv_mem[:, :, :d_xa], 1, 2).astype(BF16)
    wv = w_kv_mem[:, :, d_xa:].astype(BF16)
    kbd, vbd = _mem_kv(mem, wkt, wv)

    x2 = x.reshape(batch * seq, d)
    for l in range(depth):
        g = ln_g[l][:, None, :]
        b = ln_b[l][:, None, :]
        x2 = ffn(x2, g[0], b[0])
        j = l // n_mixers
        if l % n_mixers == 0:
            x2 = _mix_conv(x2, batch, w_in_conv[j].astype(BF16), conv_w[j], kbd, vbd, l, wo[l],
                           g[1], b[1], alpha)
        else:
            win, bg, hng = _prep_mlstm(w_in_mlstm[j], b_gates[j], head_norm_g[j], d_tok)
            x2 = _mix_mlstm(x2, batch, win, qk_conv_w[j], bg, hng, kbd, vbd, l, wo[l],
                            g[1], b[1], alpha)
        x2 = ffn(x2, g[2], b[2])
    return x2.reshape(batch, seq, d)
```

```python
import functools

import jax
import jax.numpy as jnp
from jax import lax
from jax.experimental import pallas as pl
from jax.experimental.pallas import tpu as pltpu

F32 = jnp.float32
BF16 = jnp.bfloat16

XA_HEADS = 4
CONV_WIDTH = 3
QK_CONV_WIDTH = 4
ML_HEADS = 4
LN_EPS = 1e-5

LANES = 128
SUBLANES = 8
MXU_DIM = 256
VMEM_LIMIT_BYTES = 56 * 1024 * 1024
FFN_FLAGS = None
MIXER_FLAGS = None

TM_FFN = 1024
FFN_SUB_ROWS = (768, 256)
FFN_FF_CHUNKS = (1536, 1280)
TM_MIX = 512
PROJ_CHUNK = 256
CONV_PHASE_WEIGHTS = (1, 3, 2, 4)
MLSTM_PHASE_WEIGHTS = (1, 1, 1, 1, 1, 1, 1, 0, 1, 0, 1, 1, 1, 1)
ML_CHUNK = MXU_DIM
GATE_PAD = LANES
NEG_BIG = -1e30


def _layer_norm(r, g, b):
    mu = jnp.mean(r, axis=-1, keepdims=True)
    d = r - mu
    var = jnp.mean(d * d, axis=-1, keepdims=True)
    return d * lax.rsqrt(var + LN_EPS) * g + b


def _sigmoid(x):
    return 1.0 / (1.0 + jnp.exp(-x))


def _dot(a, b):
    return jnp.dot(a, b, preferred_element_type=F32)


def _dot_nt(a, b):
    return lax.dot_general(a, b, (((1,), (1,)), ((), ())), preferred_element_type=F32)


def _dot_tn(a, b):
    return lax.dot_general(a, b, (((0,), (0,)), ((), ())), preferred_element_type=F32)


def _const_spec(shape):
    nd = len(shape)
    return pl.BlockSpec(shape, lambda *_: (0,) * nd)


def _ffn_ln_kernel(alpha, sub_rows, x_ref, wg_ref, wu_ref, wd_ref, g_ref, b_ref, *rest):
    o_ref = rest[len(rest) // 2]
    xb = x_ref[...].astype(BF16)
    hs, lo = [], 0
    for width in FFN_FF_CHUNKS:
        cols = slice(lo, lo + width)
        gate = _dot(xb, wg_ref[:, cols])
        up = _dot(xb, wu_ref[:, cols])
        hs.append((gate * _sigmoid(gate) * up).astype(BF16))
        lo += width
    h = jnp.concatenate(hs, axis=1)
    for src_ref, dst_ref in zip(rest[:len(rest) // 2], rest[len(rest) // 2 + 1:]):
        dst_ref[...] = src_ref[...].astype(BF16)
    pending, start = None, 0
    for ts in sub_rows:
        y = _dot(h[start:start + ts], wd_ref[...])
        if pending is not None:
            prows, r = pending
            o_ref[prows, :] = _layer_norm(r, g_ref[...], b_ref[...])
        rows = pl.ds(start, ts)
        pending = (rows, alpha * x_ref[rows, :] + 0.5 * y)
        start += ts
    prows, r = pending
    o_ref[prows, :] = _layer_norm(r, g_ref[...], b_ref[...])


def _row_pieces(rows, n_steps):
    return max(p for p in range(1, n_steps + 1)
               if n_steps % p == 0 and rows % p == 0 and (rows // p) % (2 * SUBLANES) == 0)


def _ffn_ln(x2, weights, g, b, alpha, casts=()):
    n, d = x2.shape
    d_ff = weights[0].shape[-1]
    assert sum(FFN_FF_CHUNKS) == d_ff and all(w % MXU_DIM == 0 for w in FFN_FF_CHUNKS)
    tm = min(TM_FFN, n)
    n_steps = n // tm
    sub_rows = FFN_SUB_ROWS if sum(FFN_SUB_ROWS) == tm else (tm,)
    in_specs = [pl.BlockSpec((tm, d), lambda i: (i, 0))]
    in_specs += [_const_spec(w.shape) for w in weights] + [_const_spec((1, d)), _const_spec((1, d))]
    out_specs = [pl.BlockSpec((tm, d), lambda i: (i, 0))]
    out_shape = [jax.ShapeDtypeStruct((n, d), F32)]
    operands = [x2, *weights, g, b]
    for w, lead in casts:
        rows, cols = w.shape[-2:]
        pieces = _row_pieces(rows, n_steps)

        def piece(i, pieces=pieces):
            return i * pieces // n_steps

        in_specs.append(pl.BlockSpec((None,) * len(lead) + (rows // pieces, cols),
                                     lambda i, piece=piece, lead=lead: (*lead, piece(i), 0)))
        out_specs.append(pl.BlockSpec((rows // pieces, cols), lambda i, piece=piece: (piece(i), 0)))
        out_shape.append(jax.ShapeDtypeStruct((rows, cols), BF16))
        operands.append(w)
    out = pl.pallas_call(
        functools.partial(_ffn_ln_kernel, alpha, sub_rows),
        grid=(n_steps,),
        in_specs=in_specs,
        out_specs=out_specs,
        out_shape=out_shape,
        compiler_params=pltpu.CompilerParams(
            dimension_semantics=("arbitrary",), vmem_limit_bytes=VMEM_LIMIT_BYTES, flags=FFN_FLAGS),
    )(*operands)
    return out[0], tuple(out[1:])


def _mem_kv_kernel(scale, mem_ref, wkt_ref, wv_ref, kbd_ref, vbd_ref):
    n_batch, n_mem, _ = mem_ref.shape
    d_xa = wv_ref.shape[-1]
    hd = d_xa // XA_HEADS
    k_shape, v_shape = kbd_ref.shape[2:], vbd_ref.shape[2:]
    k_mask = (lax.broadcasted_iota(jnp.int32, k_shape, 0) // hd
              == lax.broadcasted_iota(jnp.int32, k_shape, 1) // n_mem)
    v_mask = (lax.broadcasted_iota(jnp.int32, v_shape, 0) // n_mem
              == lax.broadcasted_iota(jnp.int32, v_shape, 1) // hd)
    for i in range(n_batch):
        memb = mem_ref[i].astype(BF16)
        kt = _dot_nt(wkt_ref[0], memb) * scale
        v = _dot(memb, wv_ref[0])
        kt4 = jnp.concatenate([kt] * XA_HEADS, axis=1)
        kbd_ref[0, i] = jnp.where(k_mask, kt4, 0.0).astype(BF16)
        v4 = jnp.concatenate([v] * XA_HEADS, axis=0)
        vbd_ref[0, i] = jnp.where(v_mask, v4, 0.0).astype(BF16)


def _mem_kv(mem, wkt, wv):
    b, n_mem, d = mem.shape
    depth, d_xa, _ = wkt.shape
    hd = d_xa // XA_HEADS
    return pl.pallas_call(
        functools.partial(_mem_kv_kernel, hd ** -0.5),
        grid=(depth,),
        in_specs=[
            pl.BlockSpec((b, n_mem, d), lambda l: (0, 0, 0)),
            pl.BlockSpec((1, d_xa, d), lambda l: (l, 0, 0)),
            pl.BlockSpec((1, d, d_xa), lambda l: (l, 0, 0)),
        ],
        out_specs=[
            pl.BlockSpec((1, b, d_xa, XA_HEADS * n_mem), lambda l: (l, 0, 0, 0)),
            pl.BlockSpec((1, b, XA_HEADS * n_mem, d_xa), lambda l: (l, 0, 0, 0)),
        ],
        out_shape=[
            jax.ShapeDtypeStruct((depth, b, d_xa, XA_HEADS * n_mem), BF16),
            jax.ShapeDtypeStruct((depth, b, XA_HEADS * n_mem, d_xa), BF16),
        ],
        compiler_params=pltpu.CompilerParams(
            dimension_semantics=("arbitrary",), vmem_limit_bytes=VMEM_LIMIT_BYTES),
    )(mem, wkt, wv)


def _cross_attention(q, kbd, vbd):
    tm, d_xa = q.shape
    n_mem = kbd.shape[1] // XA_HEADS
    hd = d_xa // XA_HEADS
    s = _dot(q.astype(BF16), kbd)
    heads = range(XA_HEADS)
    sh = [s[:, h * n_mem:(h + 1) * n_mem] for h in heads]
    mx = [jnp.max(sh[h], axis=-1, keepdims=True) for h in heads]
    p = [jnp.exp(sh[h] - mx[h]) for h in heads]
    ls = [jnp.sum(p[h], axis=-1, keepdims=True) for h in heads]
    o = _dot(jnp.concatenate([p[h].astype(BF16) for h in heads], axis=1), vbd)
    lane_head = lax.broadcasted_iota(jnp.int32, (tm, d_xa), 1) // hd
    l = jnp.broadcast_to(ls[XA_HEADS - 1], (tm, d_xa))
    for h in range(XA_HEADS - 2, -1, -1):
        l = jnp.where(lane_head == h, ls[h], l)
    return o / l


def _skewed_body(nt, x_ref, win_ref, u_a_ref, u_b_ref, reset_fn, make_phases, proj_first):
    i = pl.program_id(0)

    @pl.when(i == 0)
    def _():
        u_b_ref[...] = jnp.zeros(u_b_ref.shape, F32)

    @pl.when(lax.rem(jnp.maximum(i - 1, 0), nt) == 0)
    def _():
        reset_fn()

    width = win_ref.shape[1]
    chunks = [(lo, min(lo + PROJ_CHUNK, width)) for lo in range(0, width, PROJ_CHUNK)]
    if len(chunks) > 1 and chunks[-1][1] - chunks[-1][0] < LANES:
        chunks[-2:] = [(chunks[-2][0], width)]

    def step(u_read_ref, u_write_ref):
        xb = x_ref[...].astype(BF16)
        phases = make_phases(u_read_ref)
        total = sum(w for _, w in phases)
        done, acc = 0, 0
        for phase, weight in phases:
            acc += weight
            upto = -(-acc * len(chunks) // total)
            if not proj_first:
                phase()
            for lo, hi in chunks[done:upto]:
                u_write_ref[:, lo:hi] = _dot(xb, win_ref[:, lo:hi])
            done = max(done, upto)
            if proj_first:
                phase()

    parity = lax.rem(i, 2)

    @pl.when(parity == 0)
    def _():
        step(u_b_ref, u_a_ref)

    @pl.when(parity == 1)
    def _():
        step(u_a_ref, u_b_ref)


def _skewed_mixer_call(body, x2, batch, layer, kbd, vbd, operands, u_width, extra_scratch):
    n, d = x2.shape
    seq = n // batch
    tm = min(TM_MIX, seq)
    nt = seq // tm
    n_tiles = n // tm

    def prev(i):
        return jnp.maximum(i - 1, 0)

    in_specs = [
        pl.BlockSpec((tm, d), lambda i: (jnp.minimum(i, n_tiles - 1), 0)),
        pl.BlockSpec((tm, d), lambda i: (prev(i), 0)),
        pl.BlockSpec((1, 1) + kbd.shape[2:], lambda i: (layer, prev(i) // nt, 0, 0)),
        pl.BlockSpec((1, 1) + vbd.shape[2:], lambda i: (layer, prev(i) // nt, 0, 0)),
    ] + [_const_spec(a.shape) for a in operands]
    return pl.pallas_call(
        functools.partial(body, nt),
        grid=(n_tiles + 1,),
        in_specs=in_specs,
        out_specs=pl.BlockSpec((tm, d), lambda i: (prev(i), 0)),
        out_shape=jax.ShapeDtypeStruct((n, d), F32),
        scratch_shapes=[pltpu.VMEM((tm, u_width), F32), pltpu.VMEM((tm, u_width), F32)] + extra_scratch,
        compiler_params=pltpu.CompilerParams(
            dimension_semantics=("arbitrary",), vmem_limit_bytes=VMEM_LIMIT_BYTES, flags=MIXER_FLAGS),
    )(x2, x2, kbd, vbd, *operands)


def _causal_taps(buf_ref, cur_tile, w_ref, tm, tiles):
    width = w_ref.shape[0]
    outs = []
    for c in tiles:
        lanes = slice(c * LANES, (c + 1) * LANES)
        cur_c = cur_tile(c)
        buf_ref[c, pl.ds(SUBLANES, tm), :] = cur_c
        out = cur_c * w_ref[width - 1:width, lanes]
        for j in range(width - 1):
            off = SUBLANES - (width - 1) + j
            out = out + buf_ref[c, pl.ds(off, tm), :] * w_ref[j:j + 1, lanes]
        buf_ref[c, pl.ds(0, SUBLANES), :] = buf_ref[c, pl.ds(tm, SUBLANES), :]
        outs.append(out)
    return jnp.concatenate(outs, axis=1)


def _mix_conv_kernel(alpha, nt, x_ref, xp_ref, kbd_ref, vbd_ref, win_ref, convw_ref, wout_ref, g_ref,
                     b_ref, o_ref, u_ref, u_next_ref, zbuf_ref):
    tm = x_ref.shape[0]
    d_tok = convw_ref.shape[1]

    def reset():
        zbuf_ref[:, pl.ds(0, SUBLANES), :] = jnp.zeros((zbuf_ref.shape[0], SUBLANES, LANES), F32)

    def make_phases(u_ref):
        st = {}

        def z_tile(c):
            lo = d_tok + c * LANES
            return u_ref[:, lo:lo + LANES] * u_ref[:, d_tok + lo:d_tok + lo + LANES]

        def conv():
            conv = _causal_taps(zbuf_ref, z_tile, convw_ref, tm, range(d_tok // LANES))
            st["tok"] = (u_ref[:, :d_tok] * conv).astype(BF16)

        def attention():
            st["xa"] = _cross_attention(u_ref[:, 3 * d_tok:], kbd_ref[0, 0], vbd_ref[0, 0]).astype(BF16)

        def out_proj():
            st["mix"] = _dot(jnp.concatenate([st["tok"], st["xa"]], axis=1), wout_ref[...])

        def norm():
            o_ref[...] = _layer_norm(alpha * xp_ref[...] + st["mix"], g_ref[...], b_ref[...])

        return list(zip([conv, attention, out_proj, norm], CONV_PHASE_WEIGHTS))

    _skewed_body(nt, x_ref, win_ref, u_ref, u_next_ref, reset, make_phases, proj_first=False)


def _mix_conv(x2, batch, win, convw, kbd, vbd, layer, wout, g, b, alpha):
    d_tok = convw.shape[1]
    tm = min(TM_MIX, x2.shape[0] // batch)
    return _skewed_mixer_call(
        functools.partial(_mix_conv_kernel, alpha), x2, batch, layer, kbd, vbd,
        [win, convw, wout, g, b], win.shape[1],
        [pltpu.VMEM((d_tok // LANES, SUBLANES + tm, LANES), F32)])


def _head_window(h, dh):
    pair_start = (h // 2) * 2 * dh
    if h % 2 == 0:
        return pair_start, 0, dh
    return pair_start + 2 * dh - MXU_DIM, MXU_DIM - dh, 0


def _mix_mlstm_kernel(alpha, nt, x_ref, xp_ref, kbd_ref, vbd_ref, win_ref, convw_ref, bg_ref, hng_ref,
                      wout_ref, g_ref, b_ref, o_ref, u_ref, u_next_ref, cbuf_ref, c_ref, m_ref):
    tm = x_ref.shape[0]
    d_qk = convw_ref.shape[1]
    d_tok = d_qk // 2
    dh = d_tok // ML_HEADS
    length = ML_CHUNK
    width = MXU_DIM
    n_chunks = tm // length
    qk_tiles = d_qk // LANES
    heads = range(ML_HEADS)
    pairs = [(c, h) for c in range(n_chunks) for h in heads]

    def reset():
        cbuf_ref[:, pl.ds(0, SUBLANES), :] = jnp.zeros((cbuf_ref.shape[0], SUBLANES, LANES), F32)
        c_ref[...] = jnp.zeros(c_ref.shape, F32)
        m_ref[...] = jnp.zeros(m_ref.shape, F32)

    def make_phases(u_ref):
        st = {}
        window = [_head_window(h, dh) for h in heads]

        def rows(c):
            return slice(c * length, (c + 1) * length)

        def u_tile(c):
            return u_ref[:, c * LANES:(c + 1) * LANES]

        def conv(part):
            def phase():
                lo = part * qk_tiles // 4
                a = _causal_taps(cbuf_ref, u_tile, convw_ref, tm, range(lo, lo + qk_tiles // 4))
                a = a * _sigmoid(a)
                st["qk", part] = a.astype(BF16) if part < 2 else a * (dh ** -0.5)
                if part == 1:
                    st["q"] = jnp.concatenate([st.pop(("qk", 0)), st.pop(("qk", 1))], axis=1)
                if part == 3:
                    st["k"] = jnp.concatenate([st.pop(("qk", 2)), st.pop(("qk", 3))], axis=1)
            return phase

        def gate_math():
            gates = u_ref[:, d_qk + 2 * d_tok:d_qk + 2 * d_tok + GATE_PAD] + bg_ref[...]
            glane = lax.broadcasted_iota(jnp.int32, gates.shape, 1)
            log_sig = jnp.minimum(gates, 0.0) - jnp.log(1.0 + jnp.exp(-jnp.abs(gates)))
            gl = jnp.where(glane < ML_HEADS, gates, log_sig)
            row = lax.broadcasted_iota(jnp.int32, (length, length), 0)
            col = lax.broadcasted_iota(jnp.int32, (length, length), 1)
            st["tri"] = row >= col
            wlane = lax.broadcasted_iota(jnp.int32, (length, width), 1)
            st["valid"] = [(wlane >= first) & (wlane < first + dh) for _, first, _ in window]
            st["spare"] = [wlane == spare for _, _, spare in window]
            tri_b = st["tri"].astype(BF16)
            gcat = jnp.concatenate([gl[rows(c)] for c in range(n_chunks)], axis=1)
            cums, rest = None, gcat
            for _ in range(3):
                term = rest.astype(BF16)
                rest = rest - term.astype(F32)
                part = _dot(tri_b, term)
                cums = part if cums is None else cums + part
            for c in range(n_chunks):
                gc = gcat[:, c * GATE_PAD:(c + 1) * GATE_PAD]
                cum = cums[:, c * GATE_PAD:(c + 1) * GATE_PAD]
                gct, cumt = gc.T, cum.T
                for h in heads:
                    st["icol", c, h] = gc[:, h:h + 1]
                    st["bcol", c, h] = cum[:, ML_HEADS + h:ML_HEADS + h + 1]
                    st["irow", c, h] = gct[h:h + 1, :]
                    st["brow", c, h] = cumt[ML_HEADS + h:ML_HEADS + h + 1, :]
                    st["blast", c, h] = cum[length - 1:length, ML_HEADS + h:ML_HEADS + h + 1]

        def decay_weights():
            for c, h in pairs:
                st["logd", c, h] = jnp.where(
                    st["tri"], st["bcol", c, h] - st["brow", c, h] + st["irow", c, h], NEG_BIG)
            for c, h in pairs:
                st["a", c, h] = jnp.max(st["logd", c, h], axis=-1, keepdims=True)
            for c, h in pairs:
                st["p", c, h] = jnp.exp(st.pop(("logd", c, h)) - st["a", c, h])

        def scores():
            for c, h in pairs:
                start = window[h][0]
                st["qw", c, h] = st["q"][rows(c), start:start + width]
                st["kw", c, h] = jnp.where(st["valid"][h], st["k"][rows(c), start:start + width],
                                           0.0).astype(BF16)
            for c, h in pairs:
                s = _dot_nt(st["qw", c, h], st["kw", c, h])
                st["sc", c, h] = (s * st.pop(("p", c, h))).astype(BF16)

        def intra():
            for c, h in pairs:
                start = window[h][0]
                vw = u_ref[rows(c), d_qk + start:d_qk + start + width]
                st["vaug", c, h] = jnp.where(st["spare"][h], 1.0, vw)
            for c, h in pairs:
                st["intra", c, h] = _dot(st.pop(("sc", c, h)), st["vaug", c, h].astype(BF16))

        def recur(c):
            def combine():
                m = [m_ref[h, 0:1, 0:1] for h in heads]
                c_st = [c_ref[h] for h in heads]
                inter = [_dot(st["qw", c, h], c_st[h].astype(BF16)) for h in heads]
                log_inter = [st["bcol", c, h] + m[h] for h in heads]
                m_t = [jnp.maximum(log_inter[h], st["a", c, h]) for h in heads]
                num = [jnp.exp(st["a", c, h] - m_t[h]) * st.pop(("intra", c, h))
                       + jnp.exp(log_inter[h] - m_t[h]) * inter[h] for h in heads]
                den = [jnp.sum(jnp.where(st["spare"][h], num[h], 0.0), axis=-1, keepdims=True)
                       for h in heads]
                for h in heads:
                    st["m", h], st["c_st", h] = m[h], c_st[h]
                    st["hw", h] = num[h] / jnp.maximum(jnp.abs(den[h]), jnp.exp(-m_t[h]))

            def update():
                m = [st.pop(("m", h)) for h in heads]
                blast = [st["blast", c, h] for h in heads]
                log_w_row = [blast[h] - st["brow", c, h] + st["irow", c, h] for h in heads]
                m_new = [jnp.maximum(blast[h] + m[h], jnp.max(log_w_row[h], axis=-1, keepdims=True))
                         for h in heads]
                w_k = [jnp.exp(blast[h] - st["bcol", c, h] + st["icol", c, h] - m_new[h])
                       for h in heads]
                upd = [_dot_tn(st["kw", c, h], (w_k[h] * st.pop(("vaug", c, h))).astype(BF16))
                       for h in heads]
                for h in heads:
                    c_ref[h] = jnp.exp(blast[h] + m[h] - m_new[h]) * st.pop(("c_st", h)) + upd[h]
                    m_ref[h] = jnp.broadcast_to(m_new[h], m_ref.shape[1:])

            def head_norm():
                hw = [st.pop(("hw", h)) for h in heads]
                mu = [jnp.sum(jnp.where(st["valid"][h], hw[h], 0.0), axis=-1, keepdims=True) / dh
                      for h in heads]
                d = [jnp.where(st["valid"][h], hw[h] - mu[h], 0.0) for h in heads]
                var = [jnp.sum(d[h] * d[h], axis=-1, keepdims=True) / dh for h in heads]
                for h in heads:
                    st["win", c, h] = d[h] * lax.rsqrt(var[h] + LN_EPS)

            return [(combine, 1), (update, 0), (head_norm, 1)]

        def tail_gate():
            hn_rows = []
            for c in range(n_chunks):
                pieces = []
                for p in range(ML_HEADS // 2):
                    even, odd = st["win", c, 2 * p], st["win", c, 2 * p + 1]
                    pieces += [even[:, :LANES], even[:, LANES:] + odd[:, :LANES], odd[:, LANES:]]
                hn_rows.append(jnp.concatenate(pieces, axis=1))
            hn = hn_rows[0] if n_chunks == 1 else jnp.concatenate(hn_rows, axis=0)
            o_pre = u_ref[:, d_qk + d_tok:d_qk + 2 * d_tok]
            st["tok"] = (_sigmoid(o_pre) * (hn * hng_ref[...])).astype(BF16)

        def tail_attention():
            q_mem = u_ref[:, d_qk + 2 * d_tok + 2 * ML_HEADS:]
            xa = _cross_attention(q_mem, kbd_ref[0, 0], vbd_ref[0, 0])
            cat = jnp.concatenate([st["tok"], xa.astype(BF16)], axis=1)
            st["mix"] = _dot(cat, wout_ref[...])

        def norm():
            o_ref[...] = _layer_norm(alpha * xp_ref[...] + st["mix"], g_ref[...], b_ref[...])

        recurs = [p for c in range(n_chunks) for p, _ in recur(c)]
        phases = ([conv(part) for part in range(4)] + [gate_math, decay_weights, scores, intra]
                  + recurs + [tail_gate, tail_attention, norm])
        weights = (MLSTM_PHASE_WEIGHTS[:8] + MLSTM_PHASE_WEIGHTS[8:11] * n_chunks
                   + MLSTM_PHASE_WEIGHTS[11:])
        return list(zip(phases, weights))

    _skewed_body(nt, x_ref, win_ref, u_ref, u_next_ref, reset, make_phases, proj_first=True)


def _mix_mlstm(x2, batch, win, convw, bg, hng, kbd, vbd, layer, wout, g, b, alpha):
    d_qk = convw.shape[1]
    tm = min(TM_MIX, x2.shape[0] // batch)
    return _skewed_mixer_call(
        functools.partial(_mix_mlstm_kernel, alpha), x2, batch, layer, kbd, vbd,
        [win, convw, bg, hng, wout, g, b], win.shape[1],
        [pltpu.VMEM((d_qk // LANES, SUBLANES + tm, LANES), F32),
         pltpu.VMEM((ML_HEADS, MXU_DIM, MXU_DIM), F32),
         pltpu.VMEM((ML_HEADS, SUBLANES, LANES), F32)])


def kernel(x, mem, ln_g, ln_b, ffn_w_gate, ffn_w_up, ffn_w_down, w_kv_mem, w_out, w_in_conv, conv_w,
           w_in_mlstm, b_gates, qk_conv_w, head_norm_g):
    batch, seq, d = x.shape
    depth = ffn_w_gate.shape[0]
    d_xa = w_kv_mem.shape[-1] // 2
    d_tok = conv_w.shape[-1]
    n_mixers = 2
    alpha = (2.0 * depth) ** 0.25
    assert seq % ML_CHUNK == 0 and d_tok % ML_HEADS == 0
    assert 2 * (d_tok // ML_HEADS) % LANES == 0 and d_tok // ML_HEADS <= MXU_DIM

    ffn_params = (ffn_w_gate, ffn_w_up, ffn_w_down)
    ffn_w = tuple(w[0, 0].astype(BF16) for w in ffn_params)
    ffn_order = [(l, pos) for l in range(depth) for pos in range(2)]

    def ffn(x2, g, b, extra_casts=()):
        nonlocal ffn_w
        ffn_order.pop(0)
        casts = [(w, ffn_order[0]) for w in ffn_params] if ffn_order else []
        x2, cast = _ffn_ln(x2, ffn_w, g, b, alpha, casts + list(extra_casts))
        ffn_w = cast[:len(casts)]
        return x2, cast[len(casts):]

    wkt = jnp.swapaxes(w_kv_mem[:, :, :d_xa], 1, 2).astype(BF16)
    wv = w_kv_mem[:, :, d_xa:].astype(BF16)
    kbd, vbd = _mem_kv(mem, wkt, wv)

    x2 = x.reshape(batch * seq, d)
    for l in range(depth):
        g = ln_g[l][:, None, :]
        b = ln_b[l][:, None, :]
        j = l // n_mixers
        w_in = w_in_conv if l % n_mixers == 0 else w_in_mlstm
        x2, (win, wout) = ffn(x2, g[0], b[0], [(w_in, (j,)), (w_out, (l,))])
        if l % n_mixers == 0:
            x2 = _mix_conv(x2, batch, win, conv_w[j], kbd, vbd, l, wout, g[1], b[1], alpha)
        else:
            bg = jnp.pad(b_gates[j], (0, GATE_PAD - 2 * ML_HEADS))[None, :]
            hng = head_norm_g[j].reshape(1, d_tok)
            x2 = _mix_mlstm(x2, batch, win, qk_conv_w[j], bg, hng, kbd, vbd, l, wout, g[1], b[1], alpha)
        x2, _ = ffn(x2, g[2], b[2])
    return x2.reshape(batch, seq, d)
```

```python
import functools

import jax
import jax.numpy as jnp
from jax import lax
from jax.experimental import pallas as pl
from jax.experimental.pallas import tpu as pltpu

F32 = jnp.float32
BF16 = jnp.bfloat16

XA_HEADS = 4
ML_HEADS = 4
LN_EPS = 1e-5

LANES = 128
SUBLANES = 8
MXU_DIM = 256
VMEM_LIMIT_BYTES = 56 * 1024 * 1024

TM_FFN = 1024
FFN_SUB_ROWS = (768, 256)
FFN_FF_CHUNKS = (1536, 1280)
TM_MIX = 512
PROJ_CHUNK = 256
CONV_PHASE_WEIGHTS = (1, 3, 2, 4)
MLSTM_HEAD_WEIGHTS = (1, 1, 1, 1, 1, 1, 1, 0)
MLSTM_RECUR_WEIGHTS = (1, 0, 1)
MLSTM_TAIL_WEIGHTS = (1, 1, 0, 1)
ML_CHUNK = MXU_DIM
GATE_PAD = LANES
NEG_BIG = -1e30


def _layer_norm(r, g, b):
    mu = jnp.mean(r, axis=-1, keepdims=True)
    d = r - mu
    var = jnp.mean(d * d, axis=-1, keepdims=True)
    return d * lax.rsqrt(var + LN_EPS) * g + b


def _sigmoid(x):
    return 1.0 / (1.0 + jnp.exp(-x))


def _dot(a, b):
    return jnp.dot(a, b, preferred_element_type=F32)


def _dot_nt(a, b):
    return lax.dot_general(a, b, (((1,), (1,)), ((), ())), preferred_element_type=F32)


def _dot_tn(a, b):
    return lax.dot_general(a, b, (((0,), (0,)), ((), ())), preferred_element_type=F32)


def _const_spec(shape):
    nd = len(shape)
    return pl.BlockSpec(shape, lambda *_: (0,) * nd)


def _ffn_ln_kernel(alpha, sub_rows, x_ref, wg_ref, wu_ref, wd_ref, g_ref, b_ref, *rest):
    o_ref = rest[len(rest) // 2]
    xb = x_ref[...].astype(BF16)
    hs, lo = [], 0
    for width in FFN_FF_CHUNKS:
        cols = slice(lo, lo + width)
        gate = _dot(xb, wg_ref[:, cols])
        up = _dot(xb, wu_ref[:, cols])
        hs.append((gate * _sigmoid(gate) * up).astype(BF16))
        lo += width
    h = jnp.concatenate(hs, axis=1)
    for src_ref, dst_ref in zip(rest[:len(rest) // 2], rest[len(rest) // 2 + 1:]):
        dst_ref[...] = src_ref[...].astype(BF16)
    pending, start = None, 0
    for ts in sub_rows:
        y = _dot(h[start:start + ts], wd_ref[...])
        if pending is not None:
            prows, r = pending
            o_ref[prows, :] = _layer_norm(r, g_ref[...], b_ref[...])
        rows = pl.ds(start, ts)
        pending = (rows, alpha * x_ref[rows, :] + 0.5 * y)
        start += ts
    prows, r = pending
    o_ref[prows, :] = _layer_norm(r, g_ref[...], b_ref[...])


def _row_pieces(rows, n_steps):
    return max(p for p in range(1, n_steps + 1)
               if n_steps % p == 0 and rows % p == 0 and (rows // p) % (2 * SUBLANES) == 0)


def _ffn_ln(x2, weights, g, b, alpha, casts=()):
    n, d = x2.shape
    d_ff = weights[0].shape[-1]
    assert sum(FFN_FF_CHUNKS) == d_ff and all(w % MXU_DIM == 0 for w in FFN_FF_CHUNKS)
    tm = min(TM_FFN, n)
    n_steps = n // tm
    sub_rows = FFN_SUB_ROWS if sum(FFN_SUB_ROWS) == tm else (tm,)
    in_specs = [pl.BlockSpec((tm, d), lambda i: (i, 0))]
    in_specs += [_const_spec(w.shape) for w in weights] + [_const_spec((1, d)), _const_spec((1, d))]
    out_specs = [pl.BlockSpec((tm, d), lambda i: (i, 0))]
    out_shape = [jax.ShapeDtypeStruct((n, d), F32)]
    operands = [x2, *weights, g, b]
    for w, lead in casts:
        rows, cols = w.shape[-2:]
        pieces = _row_pieces(rows, n_steps)

        def piece(i, pieces=pieces):
            return i * pieces // n_steps

        in_specs.append(pl.BlockSpec((None,) * len(lead) + (rows // pieces, cols),
                                     lambda i, piece=piece, lead=lead: (*lead, piece(i), 0)))
        out_specs.append(pl.BlockSpec((rows // pieces, cols), lambda i, piece=piece: (piece(i), 0)))
        out_shape.append(jax.ShapeDtypeStruct((rows, cols), BF16))
        operands.append(w)
    out = pl.pallas_call(
        functools.partial(_ffn_ln_kernel, alpha, sub_rows),
        grid=(n_steps,),
        in_specs=in_specs,
        out_specs=out_specs,
        out_shape=out_shape,
        compiler_params=pltpu.CompilerParams(
            dimension_semantics=("arbitrary",), vmem_limit_bytes=VMEM_LIMIT_BYTES),
    )(*operands)
    return out[0], tuple(out[1:])


def _mem_kv_kernel(scale, mem_ref, wkt_ref, wv_ref, kbd_ref, vbd_ref):
    n_batch, n_mem, _ = mem_ref.shape
    d_xa = wv_ref.shape[-1]
    hd = d_xa // XA_HEADS
    k_shape, v_shape = kbd_ref.shape[2:], vbd_ref.shape[2:]
    k_mask = (lax.broadcasted_iota(jnp.int32, k_shape, 0) // hd
              == lax.broadcasted_iota(jnp.int32, k_shape, 1) // n_mem)
    v_mask = (lax.broadcasted_iota(jnp.int32, v_shape, 0) // n_mem
              == lax.broadcasted_iota(jnp.int32, v_shape, 1) // hd)
    for i in range(n_batch):
        memb = mem_ref[i].astype(BF16)
        kt = _dot_nt(wkt_ref[0], memb) * scale
        v = _dot(memb, wv_ref[0])
        kt4 = jnp.concatenate([kt] * XA_HEADS, axis=1)
        kbd_ref[0, i] = jnp.where(k_mask, kt4, 0.0).astype(BF16)
        v4 = jnp.concatenate([v] * XA_HEADS, axis=0)
        vbd_ref[0, i] = jnp.where(v_mask, v4, 0.0).astype(BF16)


def _mem_kv(mem, wkt, wv):
    b, n_mem, d = mem.shape
    depth, d_xa, _ = wkt.shape
    hd = d_xa // XA_HEADS
    return pl.pallas_call(
        functools.partial(_mem_kv_kernel, hd ** -0.5),
        grid=(depth,),
        in_specs=[
            pl.BlockSpec((b, n_mem, d), lambda l: (0, 0, 0)),
            pl.BlockSpec((1, d_xa, d), lambda l: (l, 0, 0)),
            pl.BlockSpec((1, d, d_xa), lambda l: (l, 0, 0)),
        ],
        out_specs=[
            pl.BlockSpec((1, b, d_xa, XA_HEADS * n_mem), lambda l: (l, 0, 0, 0)),
            pl.BlockSpec((1, b, XA_HEADS * n_mem, d_xa), lambda l: (l, 0, 0, 0)),
        ],
        out_shape=[
            jax.ShapeDtypeStruct((depth, b, d_xa, XA_HEADS * n_mem), BF16),
            jax.ShapeDtypeStruct((depth, b, XA_HEADS * n_mem, d_xa), BF16),
        ],
        compiler_params=pltpu.CompilerParams(
            dimension_semantics=("arbitrary",), vmem_limit_bytes=VMEM_LIMIT_BYTES),
    )(mem, wkt, wv)


def _cross_attention(q, kbd, vbd):
    tm, d_xa = q.shape
    n_mem = kbd.shape[1] // XA_HEADS
    hd = d_xa // XA_HEADS
    s = _dot(q.astype(BF16), kbd)
    heads = range(XA_HEADS)
    sh = [s[:, h * n_mem:(h + 1) * n_mem] for h in heads]
    mx = [jnp.max(sh[h], axis=-1, keepdims=True) for h in heads]
    p = [jnp.exp(sh[h] - mx[h]) for h in heads]
    ls = [jnp.sum(p[h], axis=-1, keepdims=True) for h in heads]
    o = _dot(jnp.concatenate([p[h].astype(BF16) for h in heads], axis=1), vbd)
    lane_head = lax.broadcasted_iota(jnp.int32, (tm, d_xa), 1) // hd
    l = jnp.broadcast_to(ls[XA_HEADS - 1], (tm, d_xa))
    for h in range(XA_HEADS - 2, -1, -1):
        l = jnp.where(lane_head == h, ls[h], l)
    return o / l


def _skewed_body(nt, x_ref, win_ref, u_a_ref, u_b_ref, reset_fn, make_phases, proj_first):
    i = pl.program_id(0)

    @pl.when(i == 0)
    def _():
        u_b_ref[...] = jnp.zeros(u_b_ref.shape, F32)

    @pl.when(lax.rem(jnp.maximum(i - 1, 0), nt) == 0)
    def _():
        reset_fn()

    width = win_ref.shape[1]
    chunks = [(lo, min(lo + PROJ_CHUNK, width)) for lo in range(0, width, PROJ_CHUNK)]
    if len(chunks) > 1 and chunks[-1][1] - chunks[-1][0] < LANES:
        chunks[-2:] = [(chunks[-2][0], width)]

    def step(u_read_ref, u_write_ref):
        xb = x_ref[...].astype(BF16)
        phases = make_phases(u_read_ref)
        total = sum(w for _, w in phases)
        done, acc = 0, 0
        for phase, weight in phases:
            acc += weight
            upto = -(-acc * len(chunks) // total)
            if not proj_first:
                phase()
            for lo, hi in chunks[done:upto]:
                u_write_ref[:, lo:hi] = _dot(xb, win_ref[:, lo:hi])
            done = max(done, upto)
            if proj_first:
                phase()

    parity = lax.rem(i, 2)

    @pl.when(parity == 0)
    def _():
        step(u_b_ref, u_a_ref)

    @pl.when(parity == 1)
    def _():
        step(u_a_ref, u_b_ref)


def _skewed_mixer_call(body, x2, batch, layer, kbd, vbd, operands, u_width, extra_scratch):
    n, d = x2.shape
    seq = n // batch
    tm = min(TM_MIX, seq)
    nt = seq // tm
    n_tiles = n // tm

    def prev(i):
        return jnp.maximum(i - 1, 0)

    in_specs = [
        pl.BlockSpec((tm, d), lambda i: (jnp.minimum(i, n_tiles - 1), 0)),
        pl.BlockSpec((tm, d), lambda i: (prev(i), 0)),
        pl.BlockSpec((1, 1) + kbd.shape[2:], lambda i: (layer, prev(i) // nt, 0, 0)),
        pl.BlockSpec((1, 1) + vbd.shape[2:], lambda i: (layer, prev(i) // nt, 0, 0)),
    ] + [_const_spec(a.shape) for a in operands]
    return pl.pallas_call(
        functools.partial(body, nt),
        grid=(n_tiles + 1,),
        in_specs=in_specs,
        out_specs=pl.BlockSpec((tm, d), lambda i: (prev(i), 0)),
        out_shape=jax.ShapeDtypeStruct((n, d), F32),
        scratch_shapes=[pltpu.VMEM((tm, u_width), F32), pltpu.VMEM((tm, u_width), F32)] + extra_scratch,
        compiler_params=pltpu.CompilerParams(
            dimension_semantics=("arbitrary",), vmem_limit_bytes=VMEM_LIMIT_BYTES),
    )(x2, x2, kbd, vbd, *operands)


def _causal_taps(buf_ref, cur_tile, w_ref, tm, tiles):
    width = w_ref.shape[0]
    outs = []
    for c in tiles:
        lanes = slice(c * LANES, (c + 1) * LANES)
        cur_c = cur_tile(c)
        buf_ref[c, pl.ds(SUBLANES, tm), :] = cur_c
        out = cur_c * w_ref[width - 1:width, lanes]
        for j in range(width - 1):
            off = SUBLANES - (width - 1) + j
            out = out + buf_ref[c, pl.ds(off, tm), :] * w_ref[j:j + 1, lanes]
        buf_ref[c, pl.ds(0, SUBLANES), :] = buf_ref[c, pl.ds(tm, SUBLANES), :]
        outs.append(out)
    return jnp.concatenate(outs, axis=1)


def _mix_conv_kernel(alpha, nt, x_ref, xp_ref, kbd_ref, vbd_ref, win_ref, convw_ref, wout_ref, g_ref,
                     b_ref, o_ref, u_ref, u_next_ref, zbuf_ref):
    tm = x_ref.shape[0]
    d_tok = convw_ref.shape[1]

    def reset():
        zbuf_ref[:, pl.ds(0, SUBLANES), :] = jnp.zeros((zbuf_ref.shape[0], SUBLANES, LANES), F32)

    def make_phases(u_ref):
        st = {}

        def z_tile(c):
            lo = d_tok + c * LANES
            return u_ref[:, lo:lo + LANES] * u_ref[:, d_tok + lo:d_tok + lo + LANES]

        def conv():
            conv = _causal_taps(zbuf_ref, z_tile, convw_ref, tm, range(d_tok // LANES))
            st["tok"] = (u_ref[:, :d_tok] * conv).astype(BF16)

        def attention():
            st["xa"] = _cross_attention(u_ref[:, 3 * d_tok:], kbd_ref[0, 0], vbd_ref[0, 0]).astype(BF16)

        def out_proj():
            st["mix"] = _dot(jnp.concatenate([st["tok"], st["xa"]], axis=1), wout_ref[...])

        def norm():
            o_ref[...] = _layer_norm(alpha * xp_ref[...] + st["mix"], g_ref[...], b_ref[...])

        return list(zip([conv, attention, out_proj, norm], CONV_PHASE_WEIGHTS))

    _skewed_body(nt, x_ref, win_ref, u_ref, u_next_ref, reset, make_phases, proj_first=False)


def _mix_conv(x2, batch, win, convw, kbd, vbd, layer, wout, g, b, alpha):
    d_tok = convw.shape[1]
    tm = min(TM_MIX, x2.shape[0] // batch)
    return _skewed_mixer_call(
        functools.partial(_mix_conv_kernel, alpha), x2, batch, layer, kbd, vbd,
        [win, convw, wout, g, b], win.shape[1],
        [pltpu.VMEM((d_tok // LANES, SUBLANES + tm, LANES), F32)])


def _head_window(h, dh):
    pair_start = (h // 2) * 2 * dh
    if h % 2 == 0:
        return pair_start, 0, dh
    return pair_start + 2 * dh - MXU_DIM, MXU_DIM - dh, 0


def _mix_mlstm_kernel(alpha, nt, x_ref, xp_ref, kbd_ref, vbd_ref, win_ref, convw_ref, bg_ref, hng_ref,
                      wout_ref, g_ref, b_ref, o_ref, u_ref, u_next_ref, cbuf_ref, c_ref, m_ref):
    tm = x_ref.shape[0]
    d_qk = convw_ref.shape[1]
    d_tok = d_qk // 2
    dh = d_tok // ML_HEADS
    length = ML_CHUNK
    width = MXU_DIM
    n_chunks = tm // length
    qk_tiles = d_qk // LANES
    heads = range(ML_HEADS)
    pairs = [(c, h) for c in range(n_chunks) for h in heads]

    def reset():
        cbuf_ref[:, pl.ds(0, SUBLANES), :] = jnp.zeros((cbuf_ref.shape[0], SUBLANES, LANES), F32)
        c_ref[...] = jnp.zeros(c_ref.shape, F32)
        m_ref[...] = jnp.zeros(m_ref.shape, F32)

    def make_phases(u_ref):
        st = {}
        window = [_head_window(h, dh) for h in heads]

        def rows(c):
            return slice(c * length, (c + 1) * length)

        def u_tile(c):
            return u_ref[:, c * LANES:(c + 1) * LANES]

        def conv(part):
            def phase():
                lo = part * qk_tiles // 4
                a = _causal_taps(cbuf_ref, u_tile, convw_ref, tm, range(lo, lo + qk_tiles // 4))
                a = a * _sigmoid(a)
                st["qk", part] = a.astype(BF16) if part < 2 else a * (dh ** -0.5)
                if part == 1:
                    st["q"] = jnp.concatenate([st.pop(("qk", 0)), st.pop(("qk", 1))], axis=1)
                if part == 3:
                    st["k"] = jnp.concatenate([st.pop(("qk", 2)), st.pop(("qk", 3))], axis=1)
            return phase

        def gate_math():
            gates = u_ref[:, d_qk + 2 * d_tok:d_qk + 2 * d_tok + GATE_PAD] + bg_ref[...]
            glane = lax.broadcasted_iota(jnp.int32, gates.shape, 1)
            log_sig = jnp.minimum(gates, 0.0) - jnp.log(1.0 + jnp.exp(-jnp.abs(gates)))
            gl = jnp.where(glane < ML_HEADS, gates, log_sig)
            row = lax.broadcasted_iota(jnp.int32, (length, length), 0)
            col = lax.broadcasted_iota(jnp.int32, (length, length), 1)
            st["tri"] = row >= col
            wlane = lax.broadcasted_iota(jnp.int32, (length, width), 1)
            st["valid"] = [(wlane >= first) & (wlane < first + dh) for _, first, _ in window]
            st["spare"] = [wlane == spare for _, _, spare in window]
            tri_b = st["tri"].astype(BF16)
            gcat = jnp.concatenate([gl[rows(c)] for c in range(n_chunks)], axis=1)
            cums, rest = None, gcat
            for _ in range(3):
                term = rest.astype(BF16)
                rest = rest - term.astype(F32)
                part = _dot(tri_b, term)
                cums = part if cums is None else cums + part
            for c in range(n_chunks):
                gc = gcat[:, c * GATE_PAD:(c + 1) * GATE_PAD]
                cum = cums[:, c * GATE_PAD:(c + 1) * GATE_PAD]
                gct, cumt = gc.T, cum.T
                for h in heads:
                    st["icol", c, h] = gc[:, h:h + 1]
                    st["bcol", c, h] = cum[:, ML_HEADS + h:ML_HEADS + h + 1]
                    st["irow", c, h] = gct[h:h + 1, :]
                    st["brow", c, h] = cumt[ML_HEADS + h:ML_HEADS + h + 1, :]
                    st["blast", c, h] = cum[length - 1:length, ML_HEADS + h:ML_HEADS + h + 1]

        def decay_weights():
            for c, h in pairs:
                st["logd", c, h] = jnp.where(
                    st["tri"], st["bcol", c, h] - st["brow", c, h] + st["irow", c, h], NEG_BIG)
            for c, h in pairs:
                st["a", c, h] = jnp.max(st["logd", c, h], axis=-1, keepdims=True)
            for c, h in pairs:
                st["p", c, h] = jnp.exp(st.pop(("logd", c, h)) - st["a", c, h])

        def scores():
            for c, h in pairs:
                start = window[h][0]
                st["qw", c, h] = st["q"][rows(c), start:start + width]
                st["kw", c, h] = jnp.where(st["valid"][h], st["k"][rows(c), start:start + width],
                                           0.0).astype(BF16)
            for c, h in pairs:
                s = _dot_nt(st["qw", c, h], st["kw", c, h])
                st["sc", c, h] = (s * st.pop(("p", c, h))).astype(BF16)

        def intra():
            for c, h in pairs:
                start = window[h][0]
                vw = u_ref[rows(c), d_qk + start:d_qk + start + width]
                st["vaug", c, h] = jnp.where(st["spare"][h], 1.0, vw)
            for c, h in pairs:
                st["intra", c, h] = _dot(st.pop(("sc", c, h)), st["vaug", c, h].astype(BF16))

        def recur(c):
            def combine():
                m = [m_ref[h, 0:1, 0:1] for h in heads]
                c_st = [c_ref[h] for h in heads]
                inter = [_dot(st["qw", c, h], c_st[h].astype(BF16)) for h in heads]
                log_inter = [st["bcol", c, h] + m[h] for h in heads]
                m_t = [jnp.maximum(log_inter[h], st["a", c, h]) for h in heads]
                num = [jnp.exp(st["a", c, h] - m_t[h]) * st.pop(("intra", c, h))
                       + jnp.exp(log_inter[h] - m_t[h]) * inter[h] for h in heads]
                den = [jnp.sum(jnp.where(st["spare"][h], num[h], 0.0), axis=-1, keepdims=True)
                       for h in heads]
                for h in heads:
                    st["m", h], st["c_st", h] = m[h], c_st[h]
                    st["hw", h] = num[h] / jnp.maximum(jnp.abs(den[h]), jnp.exp(-m_t[h]))

            def update():
                m = [st.pop(("m", h)) for h in heads]
                blast = [st["blast", c, h] for h in heads]
                log_w_row = [blast[h] - st["brow", c, h] + st["irow", c, h] for h in heads]
                m_new = [jnp.maximum(blast[h] + m[h], jnp.max(log_w_row[h], axis=-1, keepdims=True))
                         for h in heads]
                w_k = [jnp.exp(blast[h] - st["bcol", c, h] + st["icol", c, h] - m_new[h])
                       for h in heads]
                upd = [_dot_tn(st["kw", c, h], (w_k[h] * st.pop(("vaug", c, h))).astype(BF16))
                       for h in heads]
                for h in heads:
                    c_ref[h] = jnp.exp(blast[h] + m[h] - m_new[h]) * st.pop(("c_st", h)) + upd[h]
                    m_ref[h] = jnp.broadcast_to(m_new[h], m_ref.shape[1:])

            def head_norm():
                hw = [st.pop(("hw", h)) for h in heads]
                mu = [jnp.sum(jnp.where(st["valid"][h], hw[h], 0.0), axis=-1, keepdims=True) / dh
                      for h in heads]
                d = [jnp.where(st["valid"][h], hw[h] - mu[h], 0.0) for h in heads]
                var = [jnp.sum(d[h] * d[h], axis=-1, keepdims=True) / dh for h in heads]
                for h in heads:
                    st["win", c, h] = d[h] * lax.rsqrt(var[h] + LN_EPS)

            return [combine, update, head_norm]

        def tail_gate():
            hn_rows = []
            for c in range(n_chunks):
                pieces = []
                for p in range(ML_HEADS // 2):
                    even, odd = st["win", c, 2 * p], st["win", c, 2 * p + 1]
                    pieces += [even[:, :LANES], even[:, LANES:] + odd[:, :LANES], odd[:, LANES:]]
                hn_rows.append(jnp.concatenate(pieces, axis=1))
            hn = hn_rows[0] if n_chunks == 1 else jnp.concatenate(hn_rows, axis=0)
            o_pre = u_ref[:, d_qk + d_tok:d_qk + 2 * d_tok]
            st["tok"] = (_sigmoid(o_pre) * (hn * hng_ref[...])).astype(BF16)

        def attention():
            q_mem = u_ref[:, d_qk + 2 * d_tok + 2 * ML_HEADS:]
            st["xa"] = _cross_attention(q_mem, kbd_ref[0, 0], vbd_ref[0, 0]).astype(BF16)

        def out_proj():
            st["mix"] = _dot(jnp.concatenate([st["tok"], st["xa"]], axis=1), wout_ref[...])

        def norm():
            o_ref[...] = _layer_norm(alpha * xp_ref[...] + st["mix"], g_ref[...], b_ref[...])

        phases = ([conv(part) for part in range(4)] + [gate_math, decay_weights, scores, intra]
                  + [p for c in range(n_chunks) for p in recur(c)]
                  + [attention, tail_gate, out_proj, norm])
        weights = MLSTM_HEAD_WEIGHTS + MLSTM_RECUR_WEIGHTS * n_chunks + MLSTM_TAIL_WEIGHTS
        return list(zip(phases, weights))

    _skewed_body(nt, x_ref, win_ref, u_ref, u_next_ref, reset, make_phases, proj_first=True)


def _mix_mlstm(x2, batch, win, convw, bg, hng, kbd, vbd, layer, wout, g, b, alpha):
    d_qk = convw.shape[1]
    tm = min(TM_MIX, x2.shape[0] // batch)
    return _skewed_mixer_call(
        functools.partial(_mix_mlstm_kernel, alpha), x2, batch, layer, kbd, vbd,
        [win, convw, bg, hng, wout, g, b], win.shape[1],
        [pltpu.VMEM((d_qk // LANES, SUBLANES + tm, LANES), F32),
         pltpu.VMEM((ML_HEADS, MXU_DIM, MXU_DIM), F32),
         pltpu.VMEM((ML_HEADS, SUBLANES, LANES), F32)])


def kernel(x, mem, ln_g, ln_b, ffn_w_gate, ffn_w_up, ffn_w_down, w_kv_mem, w_out, w_in_conv, conv_w,
           w_in_mlstm, b_gates, qk_conv_w, head_norm_g):
    batch, seq, d = x.shape
    depth = ffn_w_gate.shape[0]
    d_xa = w_kv_mem.shape[-1] // 2
    d_tok = conv_w.shape[-1]
    n_mixers = 2
    alpha = (2.0 * depth) ** 0.25
    assert seq % ML_CHUNK == 0 and d_tok % ML_HEADS == 0
    assert 2 * (d_tok // ML_HEADS) % LANES == 0 and d_tok // ML_HEADS <= MXU_DIM

    ffn_params = (ffn_w_gate, ffn_w_up, ffn_w_down)
    ffn_w = tuple(w[0, 0].astype(BF16) for w in ffn_params)
    ffn_order = [(l, pos) for l in range(depth) for pos in range(2)]

    def ffn(x2, g, b, extra_casts=()):
        nonlocal ffn_w
        ffn_order.pop(0)
        casts = [(w, ffn_order[0]) for w in ffn_params] if ffn_order else []
        x2, cast = _ffn_ln(x2, ffn_w, g, b, alpha, casts + list(extra_casts))
        ffn_w = cast[:len(casts)]
        return x2, cast[len(casts):]

    wkt = jnp.swapaxes(w_kv_mem[:, :, :d_xa], 1, 2).astype(BF16)
    wv = w_kv_mem[:, :, d_xa:].astype(BF16)
    kbd, vbd = _mem_kv(mem, wkt, wv)

    x2 = x.reshape(batch * seq, d)
    for l in range(depth):
        g = ln_g[l][:, None, :]
        b = ln_b[l][:, None, :]
        j = l // n_mixers
        w_in = w_in_conv if l % n_mixers == 0 else w_in_mlstm
        x2, (win, wout) = ffn(x2, g[0], b[0], [(w_in, (j,)), (w_out, (l,))])
        if l % n_mixers == 0:
            x2 = _mix_conv(x2, batch, win, conv_w[j], kbd, vbd, l, wout, g[1], b[1], alpha)
        else:
            bg = jnp.pad(b_gates[j], (0, GATE_PAD - 2 * ML_HEADS))[None, :]
            hng = head_norm_g[j].reshape(1, d_tok)
            x2 = _mix_mlstm(x2, batch, win, qk_conv_w[j], bg, hng, kbd, vbd, l, wout, g[1], b[1], alpha)
        x2, _ = ffn(x2, g[2], b[2])
    return x2.reshape(batch, seq, d)
```

```python
import functools

import jax
import jax.numpy as jnp
from jax import lax
from jax.experimental import pallas as pl
from jax.experimental.pallas import tpu as pltpu

F32 = jnp.float32
BF16 = jnp.bfloat16

XA_HEADS = 4
ML_HEADS = 4
LN_EPS = 1e-5

LANES = 128
SUBLANES = 8
MXU_DIM = 256
VMEM_LIMIT_BYTES = 56 * 1024 * 1024

TM_FFN = 1024
FFN_SUB_ROWS = (768, 256)
FFN_FF_CHUNKS = (1536, 1280)
TM_MIX = 512
KV_BATCH_GROUPS = 4
PROJ_CHUNK = 256
CONV_PHASE_WEIGHTS = (1, 3, 2, 4)
MLSTM_HEAD_WEIGHTS = (1, 1, 1, 1, 1, 1, 1, 0)
MLSTM_RECUR_WEIGHTS = (1, 0, 1)
MLSTM_TAIL_WEIGHTS = (1, 1, 0, 1)
ML_CHUNK = MXU_DIM
GATE_PAD = LANES
NEG_BIG = -1e30


def _layer_norm(r, g, b):
    mu = jnp.mean(r, axis=-1, keepdims=True)
    d = r - mu
    var = jnp.mean(d * d, axis=-1, keepdims=True)
    return d * lax.rsqrt(var + LN_EPS) * g + b


def _sigmoid(x):
    return 1.0 / (1.0 + jnp.exp(-x))


def _dot(a, b):
    return jnp.dot(a, b, preferred_element_type=F32)


def _dot_nt(a, b):
    return lax.dot_general(a, b, (((1,), (1,)), ((), ())), preferred_element_type=F32)


def _dot_tn(a, b):
    return lax.dot_general(a, b, (((0,), (0,)), ((), ())), preferred_element_type=F32)


def _const_spec(shape):
    nd = len(shape)
    return pl.BlockSpec(shape, lambda *_: (0,) * nd)


def _ffn_ln_kernel(alpha, sub_rows, x_ref, wg_ref, wu_ref, wd_ref, g_ref, b_ref, *rest):
    o_ref = rest[len(rest) // 2]
    xb = x_ref[...].astype(BF16)
    hs, lo = [], 0
    for width in FFN_FF_CHUNKS:
        cols = slice(lo, lo + width)
        gate = _dot(xb, wg_ref[:, cols])
        up = _dot(xb, wu_ref[:, cols])
        hs.append((gate * _sigmoid(gate) * up).astype(BF16))
        lo += width
    h = jnp.concatenate(hs, axis=1)
    for src_ref, dst_ref in zip(rest[:len(rest) // 2], rest[len(rest) // 2 + 1:]):
        dst_ref[...] = src_ref[...].astype(BF16)
    pending, start = None, 0
    for ts in sub_rows:
        y = _dot(h[start:start + ts], wd_ref[...])
        if pending is not None:
            prows, r = pending
            o_ref[prows, :] = _layer_norm(r, g_ref[...], b_ref[...])
        rows = pl.ds(start, ts)
        pending = (rows, alpha * x_ref[rows, :] + 0.5 * y)
        start += ts
    prows, r = pending
    o_ref[prows, :] = _layer_norm(r, g_ref[...], b_ref[...])


def _row_pieces(rows, n_steps):
    return max(p for p in range(1, n_steps + 1)
               if n_steps % p == 0 and rows % p == 0 and (rows // p) % (2 * SUBLANES) == 0)


def _ffn_ln(x2, weights, g, b, alpha, casts=()):
    n, d = x2.shape
    d_ff = weights[0].shape[-1]
    assert sum(FFN_FF_CHUNKS) == d_ff and all(w % MXU_DIM == 0 for w in FFN_FF_CHUNKS)
    tm = min(TM_FFN, n)
    n_steps = n // tm
    sub_rows = FFN_SUB_ROWS if sum(FFN_SUB_ROWS) == tm else (tm,)
    in_specs = [pl.BlockSpec((tm, d), lambda i: (i, 0))]
    in_specs += [_const_spec(w.shape) for w in weights] + [_const_spec((1, d)), _const_spec((1, d))]
    out_specs = [pl.BlockSpec((tm, d), lambda i: (i, 0))]
    out_shape = [jax.ShapeDtypeStruct((n, d), F32)]
    operands = [x2, *weights, g, b]
    for w, lead in casts:
        rows, cols = w.shape[-2:]
        pieces = _row_pieces(rows, n_steps)

        def piece(i, pieces=pieces):
            return i * pieces // n_steps

        in_specs.append(pl.BlockSpec((None,) * len(lead) + (rows // pieces, cols),
                                     lambda i, piece=piece, lead=lead: (*lead, piece(i), 0)))
        out_specs.append(pl.BlockSpec((rows // pieces, cols), lambda i, piece=piece: (piece(i), 0)))
        out_shape.append(jax.ShapeDtypeStruct((rows, cols), BF16))
        operands.append(w)
    out = pl.pallas_call(
        functools.partial(_ffn_ln_kernel, alpha, sub_rows),
        grid=(n_steps,),
        in_specs=in_specs,
        out_specs=out_specs,
        out_shape=out_shape,
        compiler_params=pltpu.CompilerParams(
            dimension_semantics=("arbitrary",), vmem_limit_bytes=VMEM_LIMIT_BYTES),
    )(*operands)
    return out[0], tuple(out[1:])


def _mem_kv_kernel(scale, mem_ref, wkt_ref, wv_ref, *rest):
    n_cast = (len(rest) - 2) // 2
    kbd_ref, vbd_ref = rest[n_cast], rest[n_cast + 1]
    for src_ref, dst_ref in zip(rest[:n_cast], rest[n_cast + 2:]):
        dst_ref[...] = src_ref[...].astype(BF16)
    n_batch, n_mem, _ = mem_ref.shape
    d_xa = wv_ref.shape[-1]
    hd = d_xa // XA_HEADS
    k_shape, v_shape = kbd_ref.shape[2:], vbd_ref.shape[2:]
    k_mask = (lax.broadcasted_iota(jnp.int32, k_shape, 0) // hd
              == lax.broadcasted_iota(jnp.int32, k_shape, 1) // n_mem)
    v_mask = (lax.broadcasted_iota(jnp.int32, v_shape, 0) // n_mem
              == lax.broadcasted_iota(jnp.int32, v_shape, 1) // hd)
    for i in range(n_batch):
        memb = mem_ref[i].astype(BF16)
        kt = _dot_nt(wkt_ref[0], memb) * scale
        v = _dot(memb, wv_ref[0])
        kt4 = jnp.concatenate([kt] * XA_HEADS, axis=1)
        kbd_ref[0, i] = jnp.where(k_mask, kt4, 0.0).astype(BF16)
        v4 = jnp.concatenate([v] * XA_HEADS, axis=0)
        vbd_ref[0, i] = jnp.where(v_mask, v4, 0.0).astype(BF16)


def _mem_kv(mem, wkt, wv, casts=()):
    b, n_mem, d = mem.shape
    depth, d_xa, _ = wkt.shape
    hd = d_xa // XA_HEADS
    groups = KV_BATCH_GROUPS if b % KV_BATCH_GROUPS == 0 else 1
    bg = b // groups
    n_steps = depth * groups
    in_specs = [
        pl.BlockSpec((bg, n_mem, d), lambda l, q: (q, 0, 0)),
        pl.BlockSpec((1, d_xa, d), lambda l, q: (l, 0, 0)),
        pl.BlockSpec((1, d, d_xa), lambda l, q: (l, 0, 0)),
    ]
    out_specs = [
        pl.BlockSpec((1, bg, d_xa, XA_HEADS * n_mem), lambda l, q: (l, q, 0, 0)),
        pl.BlockSpec((1, bg, XA_HEADS * n_mem, d_xa), lambda l, q: (l, q, 0, 0)),
    ]
    out_shape = [
        jax.ShapeDtypeStruct((depth, b, d_xa, XA_HEADS * n_mem), BF16),
        jax.ShapeDtypeStruct((depth, b, XA_HEADS * n_mem, d_xa), BF16),
    ]
    operands = [mem, wkt, wv]
    for w, lead in casts:
        rows, cols = w.shape[-2:]
        pieces = _row_pieces(rows, n_steps)

        def piece(l, q, pieces=pieces):
            return (l * groups + q) * pieces // n_steps

        in_specs.append(pl.BlockSpec((None,) * len(lead) + (rows // pieces, cols),
                                     lambda l, q, piece=piece, lead=lead: (*lead, piece(l, q), 0)))
        out_specs.append(pl.BlockSpec((rows // pieces, cols), lambda l, q, piece=piece: (piece(l, q), 0)))
        out_shape.append(jax.ShapeDtypeStruct((rows, cols), BF16))
        operands.append(w)
    out = pl.pallas_call(
        functools.partial(_mem_kv_kernel, hd ** -0.5),
        grid=(depth, groups),
        in_specs=in_specs,
        out_specs=out_specs,
        out_shape=out_shape,
        compiler_params=pltpu.CompilerParams(
            dimension_semantics=("arbitrary", "arbitrary"), vmem_limit_bytes=VMEM_LIMIT_BYTES),
    )(*operands)
    return out[0], out[1], tuple(out[2:])


def _cross_attention(q, kbd, vbd):
    tm, d_xa = q.shape
    n_mem = kbd.shape[1] // XA_HEADS
    hd = d_xa // XA_HEADS
    s = _dot(q.astype(BF16), kbd)
    heads = range(XA_HEADS)
    sh = [s[:, h * n_mem:(h + 1) * n_mem] for h in heads]
    mx = [jnp.max(sh[h], axis=-1, keepdims=True) for h in heads]
    p = [jnp.exp(sh[h] - mx[h]) for h in heads]
    ls = [jnp.sum(p[h], axis=-1, keepdims=True) for h in heads]
    o = _dot(jnp.concatenate([p[h].astype(BF16) for h in heads], axis=1), vbd)
    lane_head = lax.broadcasted_iota(jnp.int32, (tm, d_xa), 1) // hd
    l = jnp.broadcast_to(ls[XA_HEADS - 1], (tm, d_xa))
    for h in range(XA_HEADS - 2, -1, -1):
        l = jnp.where(lane_head == h, ls[h], l)
    return o / l


def _skewed_body(nt, x_ref, win_ref, u_a_ref, u_b_ref, reset_fn, make_phases, proj_first):
    i = pl.program_id(0)

    @pl.when(i == 0)
    def _():
        u_b_ref[...] = jnp.zeros(u_b_ref.shape, F32)

    @pl.when(lax.rem(jnp.maximum(i - 1, 0), nt) == 0)
    def _():
        reset_fn()

    width = win_ref.shape[1]
    chunks = [(lo, min(lo + PROJ_CHUNK, width)) for lo in range(0, width, PROJ_CHUNK)]
    if len(chunks) > 1 and chunks[-1][1] - chunks[-1][0] < LANES:
        chunks[-2:] = [(chunks[-2][0], width)]

    def step(u_read_ref, u_write_ref):
        xb = x_ref[...].astype(BF16)
        phases = make_phases(u_read_ref)
        total = sum(w for _, w in phases)
        done, acc = 0, 0
        for phase, weight in phases:
            acc += weight
            upto = -(-acc * len(chunks) // total)
            if not proj_first:
                phase()
            for lo, hi in chunks[done:upto]:
                u_write_ref[:, lo:hi] = _dot(xb, win_ref[:, lo:hi])
            done = max(done, upto)
            if proj_first:
                phase()

    parity = lax.rem(i, 2)

    @pl.when(parity == 0)
    def _():
        step(u_b_ref, u_a_ref)

    @pl.when(parity == 1)
    def _():
        step(u_a_ref, u_b_ref)


def _skewed_mixer_call(body, x2, batch, layer, kbd, vbd, operands, u_width, extra_scratch):
    n, d = x2.shape
    seq = n // batch
    tm = min(TM_MIX, seq)
    nt = seq // tm
    n_tiles = n // tm

    def prev(i):
        return jnp.maximum(i - 1, 0)

    in_specs = [
        pl.BlockSpec((tm, d), lambda i: (jnp.minimum(i, n_tiles - 1), 0)),
        pl.BlockSpec((tm, d), lambda i: (prev(i), 0)),
        pl.BlockSpec((1, 1) + kbd.shape[2:], lambda i: (layer, prev(i) // nt, 0, 0)),
        pl.BlockSpec((1, 1) + vbd.shape[2:], lambda i: (layer, prev(i) // nt, 0, 0)),
    ] + [_const_spec(a.shape) for a in operands]
    return pl.pallas_call(
        functools.partial(body, nt),
        grid=(n_tiles + 1,),
        in_specs=in_specs,
        out_specs=pl.BlockSpec((tm, d), lambda i: (prev(i), 0)),
        out_shape=jax.ShapeDtypeStruct((n, d), F32),
        scratch_shapes=[pltpu.VMEM((tm, u_width), F32), pltpu.VMEM((tm, u_width), F32)] + extra_scratch,
        compiler_params=pltpu.CompilerParams(
            dimension_semantics=("arbitrary",), vmem_limit_bytes=VMEM_LIMIT_BYTES),
    )(x2, x2, kbd, vbd, *operands)


def _causal_taps(buf_ref, cur_tile, w_ref, tm, tiles):
    width = w_ref.shape[0]
    outs = []
    for c in tiles:
        lanes = slice(c * LANES, (c + 1) * LANES)
        cur_c = cur_tile(c)
        buf_ref[c, pl.ds(SUBLANES, tm), :] = cur_c
        out = cur_c * w_ref[width - 1:width, lanes]
        for j in range(width - 1):
            off = SUBLANES - (width - 1) + j
            out = out + buf_ref[c, pl.ds(off, tm), :] * w_ref[j:j + 1, lanes]
        buf_ref[c, pl.ds(0, SUBLANES), :] = buf_ref[c, pl.ds(tm, SUBLANES), :]
        outs.append(out)
    return jnp.concatenate(outs, axis=1)


def _mix_conv_kernel(alpha, nt, x_ref, xp_ref, kbd_ref, vbd_ref, win_ref, convw_ref, wout_ref, g_ref,
                     b_ref, o_ref, u_ref, u_next_ref, zbuf_ref):
    tm = x_ref.shape[0]
    d_tok = convw_ref.shape[1]

    def reset():
        zbuf_ref[:, pl.ds(0, SUBLANES), :] = jnp.zeros((zbuf_ref.shape[0], SUBLANES, LANES), F32)

    def make_phases(u_ref):
        st = {}

        def z_tile(c):
            lo = d_tok + c * LANES
            return u_ref[:, lo:lo + LANES] * u_ref[:, d_tok + lo:d_tok + lo + LANES]

        def conv():
            conv = _causal_taps(zbuf_ref, z_tile, convw_ref, tm, range(d_tok // LANES))
            st["tok"] = (u_ref[:, :d_tok] * conv).astype(BF16)

        def attention():
            st["xa"] = _cross_attention(u_ref[:, 3 * d_tok:], kbd_ref[0, 0], vbd_ref[0, 0]).astype(BF16)

        def out_proj():
            st["mix"] = _dot(jnp.concatenate([st["tok"], st["xa"]], axis=1), wout_ref[...])

        def norm():
            o_ref[...] = _layer_norm(alpha * xp_ref[...] + st["mix"], g_ref[...], b_ref[...])

        return list(zip([conv, attention, out_proj, norm], CONV_PHASE_WEIGHTS))

    _skewed_body(nt, x_ref, win_ref, u_ref, u_next_ref, reset, make_phases, proj_first=False)


def _mix_conv(x2, batch, win, convw, kbd, vbd, layer, wout, g, b, alpha):
    d_tok = convw.shape[1]
    tm = min(TM_MIX, x2.shape[0] // batch)
    return _skewed_mixer_call(
        functools.partial(_mix_conv_kernel, alpha), x2, batch, layer, kbd, vbd,
        [win, convw, wout, g, b], win.shape[1],
        [pltpu.VMEM((d_tok // LANES, SUBLANES + tm, LANES), F32)])


def _head_window(h, dh):
    pair_start = (h // 2) * 2 * dh
    if h % 2 == 0:
        return pair_start, 0, dh
    return pair_start + 2 * dh - MXU_DIM, MXU_DIM - dh, 0


def _mix_mlstm_kernel(alpha, nt, x_ref, xp_ref, kbd_ref, vbd_ref, win_ref, convw_ref, bg_ref, hng_ref,
                      wout_ref, g_ref, b_ref, o_ref, u_ref, u_next_ref, cbuf_ref, c_ref, m_ref):
    tm = x_ref.shape[0]
    d_qk = convw_ref.shape[1]
    d_tok = d_qk // 2
    dh = d_tok // ML_HEADS
    length = ML_CHUNK
    width = MXU_DIM
    n_chunks = tm // length
    qk_tiles = d_qk // LANES
    heads = range(ML_HEADS)
    pairs = [(c, h) for c in range(n_chunks) for h in heads]

    def reset():
        cbuf_ref[:, pl.ds(0, SUBLANES), :] = jnp.zeros((cbuf_ref.shape[0], SUBLANES, LANES), F32)
        c_ref[...] = jnp.zeros(c_ref.shape, F32)
        m_ref[...] = jnp.zeros(m_ref.shape, F32)

    def make_phases(u_ref):
        st = {}
        window = [_head_window(h, dh) for h in heads]

        def rows(c):
            return slice(c * length, (c + 1) * length)

        def u_tile(c):
            return u_ref[:, c * LANES:(c + 1) * LANES]

        def conv(part):
            def phase():
                lo = part * qk_tiles // 4
                a = _causal_taps(cbuf_ref, u_tile, convw_ref, tm, range(lo, lo + qk_tiles // 4))
                a = a * _sigmoid(a)
                st["qk", part] = a.astype(BF16) if part < 2 else a * (dh ** -0.5)
                if part == 1:
                    st["q"] = jnp.concatenate([st.pop(("qk", 0)), st.pop(("qk", 1))], axis=1)
                if part == 3:
                    st["k"] = jnp.concatenate([st.pop(("qk", 2)), st.pop(("qk", 3))], axis=1)
            return phase

        def gate_math():
            gates = u_ref[:, d_qk + 2 * d_tok:d_qk + 2 * d_tok + GATE_PAD] + bg_ref[...]
            glane = lax.broadcasted_iota(jnp.int32, gates.shape, 1)
            log_sig = jnp.minimum(gates, 0.0) - jnp.log(1.0 + jnp.exp(-jnp.abs(gates)))
            gl = jnp.where(glane < ML_HEADS, gates, log_sig)
            row = lax.broadcasted_iota(jnp.int32, (length, length), 0)
            col = lax.broadcasted_iota(jnp.int32, (length, length), 1)
            st["tri"] = row >= col
            wlane = lax.broadcasted_iota(jnp.int32, (length, width), 1)
            st["valid"] = [(wlane >= first) & (wlane < first + dh) for _, first, _ in window]
            st["spare"] = [wlane == spare for _, _, spare in window]
            tri_b = st["tri"].astype(BF16)
            gcat = jnp.concatenate([gl[rows(c)] for c in range(n_chunks)], axis=1)
            cums, rest = None, gcat
            for _ in range(3):
                term = rest.astype(BF16)
                rest = rest - term.astype(F32)
                part = _dot(tri_b, term)
                cums = part if cums is None else cums + part
            for c in range(n_chunks):
                gc = gcat[:, c * GATE_PAD:(c + 1) * GATE_PAD]
                cum = cums[:, c * GATE_PAD:(c + 1) * GATE_PAD]
                gct, cumt = gc.T, cum.T
                for h in heads:
                    st["icol", c, h] = gc[:, h:h + 1]
                    st["bcol", c, h] = cum[:, ML_HEADS + h:ML_HEADS + h + 1]
                    st["irow", c, h] = gct[h:h + 1, :]
                    st["brow", c, h] = cumt[ML_HEADS + h:ML_HEADS + h + 1, :]
                    st["blast", c, h] = cum[length - 1:length, ML_HEADS + h:ML_HEADS + h + 1]

        def decay_weights():
            for c, h in pairs:
                st["logd", c, h] = jnp.where(
                    st["tri"], st["bcol", c, h] - st["brow", c, h] + st["irow", c, h], NEG_BIG)
            for c, h in pairs:
                st["a", c, h] = jnp.max(st["logd", c, h], axis=-1, keepdims=True)
            for c, h in pairs:
                st["p", c, h] = jnp.exp(st.pop(("logd", c, h)) - st["a", c, h])

        def scores():
            for c, h in pairs:
                start = window[h][0]
                st["qw", c, h] = st["q"][rows(c), start:start + width]
                st["kw", c, h] = jnp.where(st["valid"][h], st["k"][rows(c), start:start + width],
                                           0.0).astype(BF16)
            for c, h in pairs:
                s = _dot_nt(st["qw", c, h], st["kw", c, h])
                st["sc", c, h] = (s * st.pop(("p", c, h))).astype(BF16)

        def intra():
            for c, h in pairs:
                start = window[h][0]
                vw = u_ref[rows(c), d_qk + start:d_qk + start + width]
                st["vaug", c, h] = jnp.where(st["spare"][h], 1.0, vw)
            for c, h in pairs:
                st["intra", c, h] = _dot(st.pop(("sc", c, h)), st["vaug", c, h].astype(BF16))

        def recur(c):
            def combine():
                m = [m_ref[h, 0:1, 0:1] for h in heads]
                c_st = [c_ref[h] for h in heads]
                inter = [_dot(st["qw", c, h], c_st[h].astype(BF16)) for h in heads]
                log_inter = [st["bcol", c, h] + m[h] for h in heads]
                m_t = [jnp.maximum(log_inter[h], st["a", c, h]) for h in heads]
                num = [jnp.exp(st["a", c, h] - m_t[h]) * st.pop(("intra", c, h))
                       + jnp.exp(log_inter[h] - m_t[h]) * inter[h] for h in heads]
                den = [jnp.sum(jnp.where(st["spare"][h], num[h], 0.0), axis=-1, keepdims=True)
                       for h in heads]
                for h in heads:
                    st["m", h], st["c_st", h] = m[h], c_st[h]
                    st["hw", h] = num[h] / jnp.maximum(jnp.abs(den[h]), jnp.exp(-m_t[h]))

            def update():
                m = [st.pop(("m", h)) for h in heads]
                blast = [st["blast", c, h] for h in heads]
                log_w_row = [blast[h] - st["brow", c, h] + st["irow", c, h] for h in heads]
                m_new = [jnp.maximum(blast[h] + m[h], jnp.max(log_w_row[h], axis=-1, keepdims=True))
                         for h in heads]
                w_k = [jnp.exp(blast[h] - st["bcol", c, h] + st["icol", c, h] - m_new[h])
                       for h in heads]
                upd = [_dot_tn(st["kw", c, h], (w_k[h] * st.pop(("vaug", c, h))).astype(BF16))
                       for h in heads]
                for h in heads:
                    c_ref[h] = jnp.exp(blast[h] + m[h] - m_new[h]) * st.pop(("c_st", h)) + upd[h]
                    m_ref[h] = jnp.broadcast_to(m_new[h], m_ref.shape[1:])

            def head_norm():
                hw = [st.pop(("hw", h)) for h in heads]
                mu = [jnp.sum(jnp.where(st["valid"][h], hw[h], 0.0), axis=-1, keepdims=True) / dh
                      for h in heads]
                d = [jnp.where(st["valid"][h], hw[h] - mu[h], 0.0) for h in heads]
                var = [jnp.sum(d[h] * d[h], axis=-1, keepdims=True) / dh for h in heads]
                for h in heads:
                    st["win", c, h] = d[h] * lax.rsqrt(var[h] + LN_EPS)

            return [combine, update, head_norm]

        def tail_gate():
            hn_rows = []
            for c in range(n_chunks):
                pieces = []
                for p in range(ML_HEADS // 2):
                    even, odd = st["win", c, 2 * p], st["win", c, 2 * p + 1]
                    pieces += [even[:, :LANES], even[:, LANES:] + odd[:, :LANES], odd[:, LANES:]]
                hn_rows.append(jnp.concatenate(pieces, axis=1))
            hn = hn_rows[0] if n_chunks == 1 else jnp.concatenate(hn_rows, axis=0)
            o_pre = u_ref[:, d_qk + d_tok:d_qk + 2 * d_tok]
            st["tok"] = (_sigmoid(o_pre) * (hn * hng_ref[...])).astype(BF16)

        def attention():
            q_mem = u_ref[:, d_qk + 2 * d_tok + 2 * ML_HEADS:]
            st["xa"] = _cross_attention(q_mem, kbd_ref[0, 0], vbd_ref[0, 0]).astype(BF16)

        def out_proj():
            st["mix"] = _dot(jnp.concatenate([st["tok"], st["xa"]], axis=1), wout_ref[...])

        def norm():
            o_ref[...] = _layer_norm(alpha * xp_ref[...] + st["mix"], g_ref[...], b_ref[...])

        phases = ([conv(part) for part in range(4)] + [gate_math, decay_weights, scores, intra]
                  + [p for c in range(n_chunks) for p in recur(c)]
                  + [attention, tail_gate, out_proj, norm])
        weights = MLSTM_HEAD_WEIGHTS + MLSTM_RECUR_WEIGHTS * n_chunks + MLSTM_TAIL_WEIGHTS
        return list(zip(phases, weights))

    _skewed_body(nt, x_ref, win_ref, u_ref, u_next_ref, reset, make_phases, proj_first=True)


def _mix_mlstm(x2, batch, win, convw, bg, hng, kbd, vbd, layer, wout, g, b, alpha):
    d_qk = convw.shape[1]
    tm = min(TM_MIX, x2.shape[0] // batch)
    return _skewed_mixer_call(
        functools.partial(_mix_mlstm_kernel, alpha), x2, batch, layer, kbd, vbd,
        [win, convw, bg, hng, wout, g, b], win.shape[1],
        [pltpu.VMEM((d_qk // LANES, SUBLANES + tm, LANES), F32),
         pltpu.VMEM((ML_HEADS, MXU_DIM, MXU_DIM), F32),
         pltpu.VMEM((ML_HEADS, SUBLANES, LANES), F32)])


def kernel(x, mem, ln_g, ln_b, ffn_w_gate, ffn_w_up, ffn_w_down, w_kv_mem, w_out, w_in_conv, conv_w,
           w_in_mlstm, b_gates, qk_conv_w, head_norm_g):
    batch, seq, d = x.shape
    depth = ffn_w_gate.shape[0]
    d_xa = w_kv_mem.shape[-1] // 2
    d_tok = conv_w.shape[-1]
    n_mixers = 2
    alpha = (2.0 * depth) ** 0.25
    assert seq % ML_CHUNK == 0 and d_tok % ML_HEADS == 0
    assert 2 * (d_tok // ML_HEADS) % LANES == 0 and d_tok // ML_HEADS <= MXU_DIM

    ffn_params = (ffn_w_gate, ffn_w_up, ffn_w_down)
    ffn_order = [(l, pos) for l in range(depth) for pos in range(2)]
    wkt = jnp.swapaxes(w_kv_mem[:, :, :d_xa], 1, 2).astype(BF16)
    wv = w_kv_mem[:, :, d_xa:].astype(BF16)
    kbd, vbd, ffn_w = _mem_kv(mem, wkt, wv, [(w, ffn_order[0]) for w in ffn_params])

    def ffn(x2, g, b, extra_casts=()):
        nonlocal ffn_w
        ffn_order.pop(0)
        casts = [(w, ffn_order[0]) for w in ffn_params] if ffn_order else []
        x2, cast = _ffn_ln(x2, ffn_w, g, b, alpha, casts + list(extra_casts))
        ffn_w = cast[:len(casts)]
        return x2, cast[len(casts):]


    x2 = x.reshape(batch * seq, d)
    for l in range(depth):
        g = ln_g[l][:, None, :]
        b = ln_b[l][:, None, :]
        j = l // n_mixers
        w_in = w_in_conv if l % n_mixers == 0 else w_in_mlstm
        x2, (win, wout) = ffn(x2, g[0], b[0], [(w_in, (j,)), (w_out, (l,))])
        if l % n_mixers == 0:
            x2 = _mix_conv(x2, batch, win, conv_w[j], kbd, vbd, l, wout, g[1], b[1], alpha)
        else:
            bg = jnp.pad(b_gates[j], (0, GATE_PAD - 2 * ML_HEADS))[None, :]
            hng = head_norm_g[j].reshape(1, d_tok)
            x2 = _mix_mlstm(x2, batch, win, qk_conv_w[j], bg, hng, kbd, vbd, l, wout, g[1], b[1], alpha)
        x2, _ = ffn(x2, g[2], b[2])
    return x2.reshape(batch, seq, d)
```
